```python
import math
import jax, jax.numpy as jnp
from jax import lax
import numpy as np

D_MODEL = 2048
BATCH = 4
SEQ = 8192
DEPTH = 1

CHUNK = 64
QBLK = 128
HEAD_DIM = 128
N_SB_HEADS = D_MODEL // 2 // HEAD_DIM
SB_WIDTH = N_SB_HEADS * HEAD_DIM
DIFF_V_DIM = 128
DIFF_QK_DIM = DIFF_V_DIM // 2
N_DIFF_HEADS = D_MODEL // 2 // DIFF_V_DIM
DIFF_WIDTH = N_DIFF_HEADS * DIFF_V_DIM
MIX_WIDTH = SB_WIDTH + DIFF_WIDTH
DIFF_QK_WIDTH = N_DIFF_HEADS * 2 * DIFF_QK_DIM
QKV_COLS = 3 * SB_WIDTH + 2 * DIFF_QK_WIDTH + DIFF_WIDTH
ROPE_THETA = 500000.0
ROT_DIM = DIFF_QK_DIM // 4
N_GROUPS = 4
EXPERTS_PER_GROUP = 8
N_EXPERTS = N_GROUPS * EXPERTS_PER_GROUP
TOP_K_INNER = 2
D_EXPERT = D_MODEL // 4
MOE_BLK = 128
PLE_DIM = 256
EPS = 1e-6

kernel_name = "hybrid_stickbreak_diffattn_hmoe_block"


def rmsnorm(x, g):
    x32 = x.astype(jnp.float32)
    y = x32 * lax.rsqrt(jnp.mean(x32 * x32, axis=-1, keepdims=True) + EPS)
    return (y * g.astype(jnp.float32)).astype(x.dtype)


def partial_rope(x, pos):
    half = ROT_DIM // 2
    inv = ROPE_THETA ** (-jnp.arange(0, ROT_DIM, 2, dtype=jnp.float32) / ROT_DIM)
    ang = pos.astype(jnp.float32)[:, None] * inv[None, :]
    cos, sin = jnp.cos(ang), jnp.sin(ang)
    x1 = x[..., :half].astype(jnp.float32)
    x2 = x[..., half:ROT_DIM].astype(jnp.float32)
    r1 = (x1 * cos - x2 * sin).astype(x.dtype)
    r2 = (x2 * cos + x1 * sin).astype(x.dtype)
    return jnp.concatenate([r1, r2, x[..., ROT_DIM:]], axis=-1)


def stick_breaking_attention(q, k, v):
    S = q.shape[2]
    scale = 1.0 / math.sqrt(HEAD_DIM)
    outs = []
    for start in range(0, S, QBLK):
        end = start + QBLK
        qb = q[:, :, start:end]
        kb = k[:, :, :end]
        vb = v[:, :, :end]
        z = jnp.einsum('bhqd,bhkd->bhqk', qb, kb).astype(jnp.float32) * scale
        t_idx = start + jnp.arange(QBLK)
        s_idx = jnp.arange(end)
        mask = s_idx[None, :] < t_idx[:, None]
        log_beta = jax.nn.log_sigmoid(z)
        log_1mb = jnp.where(mask, jax.nn.log_sigmoid(-z), 0.0)
        suffix = lax.cumsum(log_1mb, axis=3, reverse=True) - log_1mb
        a = jnp.where(mask, jnp.exp(log_beta + suffix), 0.0)
        outs.append(jnp.einsum('bhqk,bhkd->bhqd', a.astype(vb.dtype), vb))
    return jnp.concatenate(outs, axis=2)


def differential_attention(q, k, v, lam):
    S = q.shape[3]
    scale = 1.0 / math.sqrt(DIFF_QK_DIM)
    outs = []
    for start in range(0, S, QBLK):
        end = start + QBLK
        qb = q[:, :, :, start:end]
        kb = k[:, :, :, :end]
        vb = v[:, :, :end]
        s = jnp.einsum('bhmqd,bhmkd->bhmqk', qb, kb).astype(jnp.float32) * scale
        t_chunk = (start + jnp.arange(QBLK)) // CHUNK
        s_chunk = jnp.arange(end) // CHUNK
        mask = s_chunk[None, :] <= t_chunk[:, None]
        s = jnp.where(mask, s, -1e30)
        p = jax.nn.softmax(s, axis=-1)
        w = p[:, :, 0] - lam * p[:, :, 1]
        outs.append(jnp.einsum('bhqk,bhkd->bhqd', w.astype(vb.dtype), vb))
    return jnp.concatenate(outs, axis=2)


def hierarchical_moe(h, w_rg, w_re, w_gate, w_up, w_down):
    B, S, D = h.shape
    N = B * S
    xf = h.reshape(N, D)
    p_grp = jax.nn.softmax((xf @ w_rg).astype(jnp.float32), axis=-1)
    grp = jnp.argmax(p_grp, axis=-1)
    p_sel = jnp.take_along_axis(p_grp, grp[:, None], axis=1)[:, 0]
    e_logits = (xf @ w_re).astype(jnp.float32).reshape(N, N_GROUPS, EXPERTS_PER_GROUP)
    in_grp = jnp.take_along_axis(e_logits, grp[:, None, None], axis=1)[:, 0]
    top_v, top_i = lax.top_k(in_grp, TOP_K_INNER)
    gates = jax.nn.softmax(top_v, axis=-1) * p_sel[:, None]
    expert = grp[:, None] * EXPERTS_PER_GROUP + top_i

    NK = N * TOP_K_INNER
    flat_e = expert.reshape(NK)
    flat_w = gates.reshape(NK).astype(h.dtype)
    flat_tok = jnp.repeat(jnp.arange(N, dtype=jnp.int32), TOP_K_INNER)
    order = jnp.argsort(flat_e)
    se, stok, sw = flat_e[order], flat_tok[order], flat_w[order]
    counts = jnp.bincount(flat_e, length=N_EXPERTS)
    starts = jnp.cumsum(counts) - counts
    padded = ((counts + MOE_BLK - 1) // MOE_BLK) * MOE_BLK
    pends = jnp.cumsum(padded)
    pstarts = pends - padded
    dest = pstarts[se] + (jnp.arange(NK) - starts[se])
    P = NK + N_EXPERTS * MOE_BLK
    nb = P // MOE_BLK
    rows = jnp.zeros((P,), jnp.int32).at[dest].set(stok)
    wbuf = jnp.zeros((P,), h.dtype).at[dest].set(sw)
    blk_e = jnp.clip(jnp.searchsorted(pends, jnp.arange(nb) * MOE_BLK, side='right'), 0, N_EXPERTS - 1)
    xs = xf[rows].reshape(nb, MOE_BLK, D)

    def expert_block(args):
        xb, e = args
        g = xb @ w_gate[e]
        u = xb @ w_up[e]
        return (jax.nn.silu(g) * u) @ w_down[e]

    ys = lax.map(expert_block, (xs, blk_e)).reshape(P, D)
    out = jnp.zeros((N, D), h.dtype).at[rows].add(ys * wbuf[:, None])
    return out.reshape(B, S, D)


def setup_inputs(seed: int = 0) -> dict:
    key = jax.random.key(seed)
    ks = jax.random.split(key, 24)
    f32 = jnp.float32
    nrm = lambda k, shape, s: jax.random.normal(k, shape, f32) * s
    gain = lambda k, shape: 1.0 + 0.01 * jax.random.normal(k, shape, f32)
    return {
        "x": nrm(ks[0], (BATCH, SEQ, D_MODEL), 1.0),
        "p": nrm(ks[1], (DEPTH, BATCH, SEQ, PLE_DIM), 1.0),
        "g_attn": gain(ks[2], (DEPTH, D_MODEL)),
        "w_in": nrm(ks[3], (DEPTH, D_MODEL, QKV_COLS), D_MODEL ** -0.5),
        "g_sb": gain(ks[4], (DEPTH, HEAD_DIM)),
        "g_diff": gain(ks[5], (DEPTH, DIFF_V_DIM)),
        "lam_q1": nrm(ks[6], (DEPTH, DIFF_QK_DIM), 0.1),
        "lam_k1": nrm(ks[7], (DEPTH, DIFF_QK_DIM), 0.1),
        "lam_q2": nrm(ks[8], (DEPTH, DIFF_QK_DIM), 0.1),
        "lam_k2": nrm(ks[9], (DEPTH, DIFF_QK_DIM), 0.1),
        "w_out": nrm(ks[10], (DEPTH, MIX_WIDTH, D_MODEL), MIX_WIDTH ** -0.5),
        "g_moe": gain(ks[11], (DEPTH, D_MODEL)),
        "w_router_group": nrm(ks[12], (DEPTH, D_MODEL, N_GROUPS), D_MODEL ** -0.5),
        "w_router_expert": nrm(ks[13], (DEPTH, D_MODEL, N_EXPERTS), D_MODEL ** -0.5),
        "w_gate": nrm(ks[14], (DEPTH, N_EXPERTS, D_MODEL, D_EXPERT), D_MODEL ** -0.5),
        "w_up": nrm(ks[15], (DEPTH, N_EXPERTS, D_MODEL, D_EXPERT), D_MODEL ** -0.5),
        "w_down": nrm(ks[16], (DEPTH, N_EXPERTS, D_EXPERT, D_MODEL), D_EXPERT ** -0.5),
        "g_ple": gain(ks[17], (DEPTH, D_MODEL)),
        "w_ple": nrm(ks[18], (DEPTH, PLE_DIM, D_MODEL), PLE_DIM ** -0.5),
        "w_ple_gate": nrm(ks[19], (DEPTH, D_MODEL, D_MODEL), D_MODEL ** -0.5),
        "g_final": gain(ks[20], (D_MODEL,)),
    }


def reference(x, p, g_attn, w_in, g_sb, g_diff, lam_q1, lam_k1, lam_q2, lam_k2, w_out,
              g_moe, w_router_group, w_router_expert, w_gate, w_up, w_down,
              g_ple, w_ple, w_ple_gate, g_final):
    B, S, _ = x.shape
    pos = jnp.arange(S, dtype=jnp.int32)
    h = x
    for i in range(DEPTH):
        a = rmsnorm(h, g_attn[i])
        qkv = a @ w_in[i]
        o = 0
        q_sb = qkv[..., o:o + SB_WIDTH]; o += SB_WIDTH
        k_sb = qkv[..., o:o + SB_WIDTH]; o += SB_WIDTH
        v_sb = qkv[..., o:o + SB_WIDTH]; o += SB_WIDTH
        q_df = qkv[..., o:o + DIFF_QK_WIDTH]; o += DIFF_QK_WIDTH
        k_df = qkv[..., o:o + DIFF_QK_WIDTH]; o += DIFF_QK_WIDTH
        v_df = qkv[..., o:o + DIFF_WIDTH]

        heads = lambda t, H, d: t.reshape(B, S, H, d).transpose(0, 2, 1, 3)
        o_sb = stick_breaking_attention(heads(q_sb, N_SB_HEADS, HEAD_DIM),
                                        heads(k_sb, N_SB_HEADS, HEAD_DIM),
                                        heads(v_sb, N_SB_HEADS, HEAD_DIM))
        o_sb = rmsnorm(o_sb, g_sb[i]).transpose(0, 2, 1, 3).reshape(B, S, SB_WIDTH)

        qk2 = lambda t: t.reshape(B, S, N_DIFF_HEADS, 2, DIFF_QK_DIM).transpose(0, 2, 3, 1, 4)
        qd = partial_rope(qk2(q_df), pos)
        kd = partial_rope(qk2(k_df), pos)
        vd = heads(v_df, N_DIFF_HEADS, DIFF_V_DIM)
        lam_init = 0.8 - 0.6 * math.exp(-0.3 * i)
        lam = (jnp.exp(jnp.sum(lam_q1[i].astype(jnp.float32) * lam_k1[i].astype(jnp.float32)))
               - jnp.exp(jnp.sum(lam_q2[i].astype(jnp.float32) * lam_k2[i].astype(jnp.float32)))
               + lam_init)
        o_df = differential_attention(qd, kd, vd, lam)
        o_df = (rmsnorm(o_df, g_diff[i]) * (1.0 - lam_init)).astype(h.dtype)
        o_df = o_df.transpose(0, 2, 1, 3).reshape(B, S, DIFF_WIDTH)

        h = h + jnp.concatenate([o_sb, o_df], axis=-1) @ w_out[i]

        m = rmsnorm(h, g_moe[i])
        h = h + hierarchical_moe(m, w_router_group[i], w_router_expert[i],
                                 w_gate[i], w_up[i], w_down[i])

        e = rmsnorm(h, g_ple[i])
        h = h + (p[i] @ w_ple[i]) * jax.nn.sigmoid(e @ w_ple_gate[i])
    return rmsnorm(h, g_final)
```

```python
import functools
import math

import jax
import jax.numpy as jnp
from jax import lax
from jax.experimental import pallas as pl
from jax.experimental.pallas import tpu as pltpu

EPS = 1e-6
HEAD_DIM = 128
DIFF_QK_DIM = 64
CHUNK = 64
ROPE_THETA = 500000.0
ROT_DIM = DIFF_QK_DIM // 4
TOP_K_INNER = 2
LANES = 128
V7X_VMEM_LIMIT = 56 * 1024 * 1024

F32 = jnp.float32
BF16 = jnp.bfloat16


def _pick(n, prefs):
    for t in prefs:
        if n % t == 0:
            return t
    raise ValueError(f"no tile in {prefs} divides {n}")


def _rms_scale(x):
    return lax.rsqrt(jnp.mean(x * x, axis=-1, keepdims=True) + EPS)


def _qkv_kernel(x_ref, g_ref, w_ref, cos_ref, sina_ref, sinb_ref, o_ref, a_ref, *,
                blocks_per_region, sb_scale, df_scale):
    j = pl.program_id(1)

    @pl.when(j == 0)
    def _():
        x = x_ref[...]
        a_ref[...] = (x * _rms_scale(x) * g_ref[...]).astype(BF16)

    acc = jnp.dot(a_ref[...], w_ref[...], preferred_element_type=F32)
    region = j // blocks_per_region
    tn = acc.shape[1]

    def rope_store(scale):
        for c in range(tn // LANES):
            blk = acc[:, c * LANES:(c + 1) * LANES]
            r = (blk * cos_ref[...]
                 + pltpu.roll(blk, LANES - ROT_DIM // 2, 1) * sina_ref[...]
                 + pltpu.roll(blk, ROT_DIM // 2, 1) * sinb_ref[...])
            o_ref[:, c * LANES:(c + 1) * LANES] = (r * scale).astype(o_ref.dtype)

    @pl.when(region == 0)
    def _():
        o_ref[...] = (acc * sb_scale).astype(o_ref.dtype)

    @pl.when(region == 3)
    def _():
        rope_store(df_scale)

    @pl.when(region == 4)
    def _():
        rope_store(1.0)

    @pl.when((region == 1) | (region == 2) | (region == 5))
    def _():
        o_ref[...] = acc.astype(o_ref.dtype)


def _rope_tables(seq):
    half = ROT_DIM // 2
    inv = ROPE_THETA ** (-jnp.arange(0, ROT_DIM, 2, dtype=F32) / ROT_DIM)
    ang = jnp.arange(seq, dtype=F32)[:, None] * inv[None, :]
    cos, sin = jnp.cos(ang), jnp.sin(ang)
    comp = jnp.concatenate([cos, cos, jnp.ones((seq, DIFF_QK_DIM - ROT_DIM), F32)], axis=1)
    zeros_h = jnp.zeros((seq, half), F32)
    zeros_r = jnp.zeros((seq, DIFF_QK_DIM - ROT_DIM), F32)
    sina = jnp.concatenate([-sin, zeros_h, zeros_r], axis=1)
    sinb = jnp.concatenate([zeros_h, sin, zeros_r], axis=1)
    rep = LANES // DIFF_QK_DIM
    return jnp.tile(comp, (1, rep)), jnp.tile(sina, (1, rep)), jnp.tile(sinb, (1, rep))


def _qkv_proj(x2, g, w_bf16, tables, seq):
    n, d = x2.shape
    cols = w_bf16.shape[1]
    region = cols // 6
    tm = _pick(seq, (1024, 512, 256, 128))
    tn = _pick(region, (512, 256, 128))
    pos_blocks = seq // tm
    kern = functools.partial(_qkv_kernel, blocks_per_region=region // tn,
                             sb_scale=1.0 / math.sqrt(HEAD_DIM), df_scale=1.0 / math.sqrt(DIFF_QK_DIM))
    tab_spec = pl.BlockSpec((tm, LANES), lambda i, j: (i % pos_blocks, 0))
    return pl.pallas_call(
        kern,
        grid=(n // tm, cols // tn),
        in_specs=[pl.BlockSpec((tm, d), lambda i, j: (i, 0)),
                  pl.BlockSpec((1, d), lambda i, j: (0, 0)),
                  pl.BlockSpec((d, tn), lambda i, j: (0, j)),
                  tab_spec, tab_spec, tab_spec],
        out_specs=pl.BlockSpec((tm, tn), lambda i, j: (i, j)),
        out_shape=jax.ShapeDtypeStruct((n, cols), BF16),
        scratch_shapes=[pltpu.VMEM((tm, d), BF16)],
        compiler_params=pltpu.CompilerParams(dimension_semantics=("parallel", "arbitrary"),
                                             vmem_limit_bytes=V7X_VMEM_LIMIT),
        name="qkv_proj",
    )(x2, g, w_bf16, *tables)


def _sb_attn_kernel(q_ref, k_ref, v_ref, u_ref, g_ref, o_ref, acc_ref, carry_ref, *, bq):
    i = pl.program_id(2)
    q = q_ref[0]
    tri = u_ref[...]

    def tile(j, diagonal):
        off = pl.multiple_of(j * bq, bq)
        k = k_ref[0, pl.ds(off, bq), :]
        v = v_ref[0, pl.ds(off, bq), :]
        z = lax.dot_general(q, k, (((1,), (1,)), ((), ())), preferred_element_type=F32)
        lse = jnp.log(1.0 + jnp.exp(-jnp.abs(z)))
        neg_log_1mb = jnp.maximum(z, 0.0) + lse
        log_beta = jnp.minimum(z, 0.0) - lse
        if diagonal:
            row = lax.broadcasted_iota(jnp.int32, (bq, bq), 0)
            col = lax.broadcasted_iota(jnp.int32, (bq, bq), 1)
            mask = col < row
            neg_log_1mb = jnp.where(mask, neg_log_1mb, 0.0)
        suffix = jnp.dot(neg_log_1mb.astype(BF16), tri, preferred_element_type=F32)
        a = jnp.exp(log_beta - suffix - carry_ref[...])
        if diagonal:
            a = jnp.where(mask, a, 0.0)
        acc_ref[...] += jnp.dot(a.astype(BF16), v, preferred_element_type=F32)
        carry_ref[...] += jnp.sum(neg_log_1mb, axis=-1, keepdims=True)

    acc_ref[...] = jnp.zeros_like(acc_ref)
    carry_ref[...] = jnp.zeros_like(carry_ref)
    tile(i, True)

    def body(jj, c):
        tile(i - 1 - jj, False)
        return c

    lax.fori_loop(0, i, body, 0)
    acc = acc_ref[...]
    o_ref[0] = (acc * _rms_scale(acc) * g_ref[...]).astype(o_ref.dtype)


def _sb_attention(qkv3, tri, g, *, n_heads, col0, bq):
    b, s, _ = qkv3.shape
    kern = functools.partial(_sb_attn_kernel, bq=bq)
    return pl.pallas_call(
        kern,
        grid=(b, n_heads, s // bq),
        in_specs=[pl.BlockSpec((1, bq, HEAD_DIM), lambda bi, h, i: (bi, i, col0 + h)),
                  pl.BlockSpec((1, s, HEAD_DIM), lambda bi, h, i: (bi, 0, col0 + n_heads + h)),
                  pl.BlockSpec((1, s, HEAD_DIM), lambda bi, h, i: (bi, 0, col0 + 2 * n_heads + h)),
                  pl.BlockSpec((bq, bq), lambda bi, h, i: (0, 0)),
                  pl.BlockSpec((1, HEAD_DIM), lambda bi, h, i: (0, 0))],
        out_specs=pl.BlockSpec((1, bq, HEAD_DIM), lambda bi, h, i: (bi, i, h)),
        out_shape=jax.ShapeDtypeStruct((b, s, n_heads * HEAD_DIM), BF16),
        scratch_shapes=[pltpu.VMEM((bq, HEAD_DIM), F32), pltpu.VMEM((bq, 1), F32)],
        compiler_params=pltpu.CompilerParams(dimension_semantics=("parallel", "parallel", "arbitrary")),
        name="sb_attention",
    )(qkv3, qkv3, qkv3, tri, g)


def _diff_attn_kernel(q_ref, k_ref, v_ref, lq1_ref, lk1_ref, lq2_ref, lk2_ref, g_ref, o_ref,
                      acc0_ref, acc1_ref, m0_ref, l0_ref, m1_ref, l1_ref, *, bq, lam_init):
    i = pl.program_id(2)
    q = q_ref[0]
    lane = lax.broadcasted_iota(jnp.int32, q.shape, 1)
    zero = jnp.zeros_like(q)
    q_maps = (jnp.where(lane < DIFF_QK_DIM, q, zero), jnp.where(lane >= DIFF_QK_DIM, q, zero))
    state = ((acc0_ref, m0_ref, l0_ref), (acc1_ref, m1_ref, l1_ref))

    def tile(j, diagonal):
        off = pl.multiple_of(j * bq, bq)
        k = k_ref[0, pl.ds(off, bq), :]
        v = v_ref[0, pl.ds(off, bq), :]
        if diagonal:
            row = lax.broadcasted_iota(jnp.int32, (bq, bq), 0)
            col = lax.broadcasted_iota(jnp.int32, (bq, bq), 1)
            mask = (col // CHUNK) <= (row // CHUNK)
        for qm, (acc_ref, m_ref, l_ref) in zip(q_maps, state):
            s = lax.dot_general(qm, k, (((1,), (1,)), ((), ())), preferred_element_type=F32)
            if diagonal:
                s = jnp.where(mask, s, -1e30)
            m_prev = m_ref[...]
            m_new = jnp.maximum(m_prev, jnp.max(s, axis=-1, keepdims=True))
            alpha = jnp.exp(m_prev - m_new)
            p = jnp.exp(s - m_new)
            l_ref[...] = alpha * l_ref[...] + jnp.sum(p, axis=-1, keepdims=True)
            acc_ref[...] = alpha * acc_ref[...] + jnp.dot(p.astype(BF16), v, preferred_element_type=F32)
            m_ref[...] = m_new

    for acc_ref, m_ref, l_ref in state:
        acc_ref[...] = jnp.zeros_like(acc_ref)
        m_ref[...] = jnp.full_like(m_ref, -1e30)
        l_ref[...] = jnp.zeros_like(l_ref)
    tile(i, True)

    def body(j, c):
        tile(j, False)
        return c

    lax.fori_loop(0, i, body, 0)
    lam = (jnp.exp(jnp.sum(lq1_ref[...] * lk1_ref[...], axis=-1, keepdims=True))
           - jnp.exp(jnp.sum(lq2_ref[...] * lk2_ref[...], axis=-1, keepdims=True)) + lam_init)
    o = acc0_ref[...] / l0_ref[...] - lam * (acc1_ref[...] / l1_ref[...])
    o_ref[0] = (o * _rms_scale(o) * g_ref[...] * (1.0 - lam_init)).astype(o_ref.dtype)


def _diff_attention(qkv3, lams, g, *, n_heads, col0, bq, lam_init):
    b, s, _ = qkv3.shape
    kern = functools.partial(_diff_attn_kernel, bq=bq, lam_init=lam_init)
    vec = lambda w: pl.BlockSpec((1, w), lambda bi, h, i: (0, 0))
    col = lambda c: pltpu.VMEM((bq, c), F32)
    return pl.pallas_call(
        kern,
        grid=(b, n_heads, s // bq),
        in_specs=[pl.BlockSpec((1, bq, HEAD_DIM), lambda bi, h, i: (bi, i, col0 + h)),
                  pl.BlockSpec((1, s, HEAD_DIM), lambda bi, h, i: (bi, 0, col0 + n_heads + h)),
                  pl.BlockSpec((1, s, HEAD_DIM), lambda bi, h, i: (bi, 0, col0 + 2 * n_heads + h)),
                  vec(DIFF_QK_DIM), vec(DIFF_QK_DIM), vec(DIFF_QK_DIM), vec(DIFF_QK_DIM), vec(HEAD_DIM)],
        out_specs=pl.BlockSpec((1, bq, HEAD_DIM), lambda bi, h, i: (bi, i, h)),
        out_shape=jax.ShapeDtypeStruct((b, s, n_heads * HEAD_DIM), BF16),
        scratch_shapes=[col(HEAD_DIM), col(HEAD_DIM), col(1), col(1), col(1), col(1)],
        compiler_params=pltpu.CompilerParams(dimension_semantics=("parallel", "parallel", "arbitrary")),
        name="diff_attention",
    )(qkv3, qkv3, qkv3, *lams, g)


def _outproj_kernel(x_ref, osb_ref, odf_ref, w_ref, g_ref, wr_ref, h_ref, m_ref, lg_ref, *, sb_width):
    h = (x_ref[...]
         + jnp.dot(osb_ref[...], w_ref[:sb_width, :], preferred_element_type=F32)
         + jnp.dot(odf_ref[...], w_ref[sb_width:, :], preferred_element_type=F32))
    h_ref[...] = h
    m = h * _rms_scale(h) * g_ref[...]
    m_ref[...] = m
    lg_ref[...] = jnp.dot(m, wr_ref[...], preferred_element_type=F32, precision=lax.Precision.HIGHEST)


def _outproj_router(x2, o_sb, o_df, w_out_bf16, g_moe, w_router):
    n, d = x2.shape
    sbw, dfw = o_sb.shape[1], o_df.shape[1]
    tm = _pick(n, (512, 256, 128))
    row = lambda w: pl.BlockSpec((tm, w), lambda i: (i, 0))
    full = lambda a: pl.BlockSpec(a.shape, lambda i: (0, 0))
    return pl.pallas_call(
        functools.partial(_outproj_kernel, sb_width=sbw),
        grid=(n // tm,),
        in_specs=[row(d), row(sbw), row(dfw), full(w_out_bf16), full(g_moe), full(w_router)],
        out_specs=[row(d), row(d), row(LANES)],
        out_shape=[jax.ShapeDtypeStruct((n, d), F32), jax.ShapeDtypeStruct((n, d), F32),
                   jax.ShapeDtypeStruct((n, LANES), F32)],
        compiler_params=pltpu.CompilerParams(dimension_semantics=("parallel",),
                                             vmem_limit_bytes=V7X_VMEM_LIMIT),
        name="outproj_router",
    )(x2, o_sb, o_df, w_out_bf16, g_moe, w_router)


def _gather_rows(idx_ref, src_hbm, dst_ref, sem, count):
    def issue(r, c):
        pltpu.make_async_copy(src_hbm.at[pl.ds(idx_ref[0, 0, r], 1), :], dst_ref.at[pl.ds(r, 1), :], sem).start()
        return c

    lax.fori_loop(0, count, issue, 0)
    pltpu.make_async_copy(src_hbm.at[pl.ds(0, count), :], dst_ref, sem).wait()


def _expert_kernel(be_ref, rows_ref, m_hbm, wg_ref, wu_ref, wd_ref, y_ref, x_buf, sem):
    del be_ref
    _gather_rows(rows_ref, m_hbm, x_buf, sem, x_buf.shape[0])
    xb = x_buf[...].astype(BF16)
    g = jnp.dot(xb, wg_ref[0], preferred_element_type=F32)
    u = jnp.dot(xb, wu_ref[0], preferred_element_type=F32)
    hmid = g * (1.0 / (1.0 + jnp.exp(-g))) * u
    y_ref[...] = jnp.dot(hmid.astype(BF16), wd_ref[0], preferred_element_type=F32)


def _expert_ffn(m, rows3, blk_e, wg, wu, wd):
    n, d = m.shape
    nb, _, blk = rows3.shape
    de = wg.shape[2]
    grid_spec = pltpu.PrefetchScalarGridSpec(
        num_scalar_prefetch=1,
        grid=(nb,),
        in_specs=[pl.BlockSpec((1, 1, blk), lambda i, be: (i, 0, 0), memory_space=pltpu.SMEM),
                  pl.BlockSpec(memory_space=pl.ANY),
                  pl.BlockSpec((1, d, de), lambda i, be: (be[i], 0, 0)),
                  pl.BlockSpec((1, d, de), lambda i, be: (be[i], 0, 0)),
                  pl.BlockSpec((1, de, d), lambda i, be: (be[i], 0, 0))],
        out_specs=pl.BlockSpec((blk, d), lambda i, be: (i, 0)),
        scratch_shapes=[pltpu.VMEM((blk, d), F32), pltpu.SemaphoreType.DMA(())],
    )
    return pl.pallas_call(
        _expert_kernel,
        grid_spec=grid_spec,
        out_shape=jax.ShapeDtypeStruct((nb * blk, d), F32),
        compiler_params=pltpu.CompilerParams(dimension_semantics=("arbitrary",),
                                             vmem_limit_bytes=V7X_VMEM_LIMIT),
        name="expert_ffn",
    )(blk_e, rows3, m, wg, wu, wd)


def _combine_kernel(pa_ref, pb_ref, ys_hbm, h_ref, gates_ref, p_ref, gple_ref, wple_ref, wgate_ref, gfin_ref,
                    o_ref, ya_buf, yb_buf, sem_a, sem_b, *, final):
    tm = h_ref.shape[0]
    _gather_rows(pa_ref, ys_hbm, ya_buf, sem_a, tm)
    _gather_rows(pb_ref, ys_hbm, yb_buf, sem_b, tm)
    gates = gates_ref[...]
    h = h_ref[...] + gates[:, 0:1] * ya_buf[...] + gates[:, 1:2] * yb_buf[...]
    e = (h * _rms_scale(h) * gple_ref[...]).astype(BF16)
    z = jnp.dot(e, wgate_ref[...], preferred_element_type=F32)
    pe = jnp.dot(p_ref[...].astype(BF16), wple_ref[...], preferred_element_type=F32)
    h = h + pe * (1.0 / (1.0 + jnp.exp(-z)))
    if final:
        h = h * _rms_scale(h) * gfin_ref[...]
    o_ref[...] = h


def _combine_ple(ys, pos_a, pos_b, h1, gates, p2, g_ple, w_ple, w_ple_gate, g_final, *, final):
    n, d = h1.shape
    nt, _, tm = pos_a.shape
    row = lambda w: pl.BlockSpec((tm, w), lambda i: (i, 0))
    full = lambda a: pl.BlockSpec(a.shape, lambda i: (0, 0))
    idx = pl.BlockSpec((1, 1, tm), lambda i: (i, 0, 0), memory_space=pltpu.SMEM)
    return pl.pallas_call(
        functools.partial(_combine_kernel, final=final),
        grid=(nt,),
        in_specs=[idx, idx, pl.BlockSpec(memory_space=pl.ANY), row(d), row(gates.shape[1]), row(p2.shape[1]),
                  full(g_ple), full(w_ple), full(w_ple_gate), full(g_final)],
        out_specs=row(d),
        out_shape=jax.ShapeDtypeStruct((n, d), F32),
        scratch_shapes=[pltpu.VMEM((tm, d), F32), pltpu.VMEM((tm, d), F32),
                        pltpu.SemaphoreType.DMA(()), pltpu.SemaphoreType.DMA(())],
        compiler_params=pltpu.CompilerParams(dimension_semantics=("arbitrary",),
                                             vmem_limit_bytes=V7X_VMEM_LIMIT),
        name="combine_ple",
    )(pos_a, pos_b, ys, h1, gates, p2, g_ple, w_ple, w_ple_gate, g_final)


def _route(logits, n_groups, n_experts, blk):
    n = logits.shape[0]
    epg = n_experts // n_groups
    p_grp = jax.nn.softmax(logits[:, :n_groups], axis=-1)
    grp = jnp.argmax(p_grp, axis=-1)
    p_sel = jnp.take_along_axis(p_grp, grp[:, None], axis=1)[:, 0]
    e_logits = logits[:, n_groups:n_groups + n_experts].reshape(n, n_groups, epg)
    in_grp = jnp.take_along_axis(e_logits, grp[:, None, None], axis=1)[:, 0]
    top_v, top_i = lax.top_k(in_grp, TOP_K_INNER)
    gates = jax.nn.softmax(top_v, axis=-1) * p_sel[:, None]
    expert = (grp[:, None] * epg + top_i).astype(jnp.int32)

    nk = n * TOP_K_INNER
    flat_e = expert.reshape(nk)
    onehot = (flat_e[:, None] == jnp.arange(n_experts, dtype=jnp.int32)[None, :]).astype(jnp.int32)
    csum = jnp.cumsum(onehot, axis=0)
    rank = jnp.take_along_axis(csum, flat_e[:, None], axis=1)[:, 0] - 1
    counts = csum[-1]
    padded = ((counts + blk - 1) // blk) * blk
    pends = jnp.cumsum(padded)
    pstarts = pends - padded
    dest = (pstarts[flat_e] + rank).astype(jnp.int32)
    total = nk + n_experts * blk
    nb = total // blk
    flat_tok = jnp.repeat(jnp.arange(n, dtype=jnp.int32), TOP_K_INNER)
    rows = jnp.zeros((total,), jnp.int32).at[dest].set(flat_tok)
    blk_e = jnp.clip(jnp.searchsorted(pends, jnp.arange(nb, dtype=jnp.int32) * blk, side='right'),
                     0, n_experts - 1).astype(jnp.int32)
    return gates, dest.reshape(n, TOP_K_INNER), rows.reshape(nb, 1, blk), blk_e


def kernel(x, p, g_attn, w_in, g_sb, g_diff, lam_q1, lam_k1, lam_q2, lam_k2, w_out, g_moe, w_router_group,
           w_router_expert, w_gate, w_up, w_down, g_ple, w_ple, w_ple_gate, g_final):
    b, s, d = x.shape
    n = b * s
    depth = p.shape[0]
    sb_width = d // 2
    n_heads = sb_width // HEAD_DIM
    n_groups = w_router_group.shape[2]
    n_experts = w_router_expert.shape[2]
    assert n_groups + n_experts <= LANES and w_in.shape[2] == 6 * sb_width
    bq = _pick(s, (256, 128))
    moe_blk = 128
    tm_c = _pick(n, (256, 128))

    tables = _rope_tables(s)
    tri = (jnp.arange(bq)[:, None] > jnp.arange(bq)[None, :]).astype(BF16)
    row2 = lambda a: a.reshape(1, -1)
    blocks = sb_width // HEAD_DIM

    h = x.reshape(n, d)
    for i in range(depth):
        qkv = _qkv_proj(h, row2(g_attn[i]), w_in[i].astype(BF16), tables, s)
        qkv3 = qkv.reshape(b, s, -1)
        o_sb = _sb_attention(qkv3, tri, row2(g_sb[i]), n_heads=n_heads, col0=0, bq=bq)
        lam_init = 0.8 - 0.6 * math.exp(-0.3 * i)
        lams = (row2(lam_q1[i]), row2(lam_k1[i]), row2(lam_q2[i]), row2(lam_k2[i]))
        o_df = _diff_attention(qkv3, lams, row2(g_diff[i]), n_heads=n_heads, col0=3 * blocks, bq=bq,
                               lam_init=lam_init)
        w_router = jnp.concatenate(
            [w_router_group[i], w_router_expert[i], jnp.zeros((d, LANES - n_groups - n_experts), F32)], axis=1)
        h1, m, logits = _outproj_router(h, o_sb.reshape(n, -1), o_df.reshape(n, -1), w_out[i].astype(BF16),
                                        row2(g_moe[i]), w_router)
        gates, pos, rows3, blk_e = _route(logits, n_groups, n_experts, moe_blk)
        ys = _expert_ffn(m, rows3, blk_e, w_gate[i].astype(BF16), w_up[i].astype(BF16), w_down[i].astype(BF16))
        pos_a = pos[:, 0].reshape(n // tm_c, 1, tm_c)
        pos_b = pos[:, 1].reshape(n // tm_c, 1, tm_c)
        h = _combine_ple(ys, pos_a, pos_b, h1, gates, p[i].reshape(n, -1), row2(g_ple[i]), w_ple[i].astype(BF16),
                         w_ple_gate[i].astype(BF16), row2(g_final), final=(i == depth - 1))
    return h.reshape(b, s, d)
```

```python
import functools
import math

import jax
import jax.numpy as jnp
from jax import lax
from jax.experimental import pallas as pl
from jax.experimental.pallas import tpu as pltpu

EPS = 1e-6
HEAD_DIM = 128
DIFF_QK_DIM = 64
CHUNK = 64
ROPE_THETA = 500000.0
ROT_DIM = DIFF_QK_DIM // 4
TOP_K_INNER = 2
LOG2E = 1.4426950408889634
LANES = 128
V7X_VMEM_LIMIT = 56 * 1024 * 1024

F32 = jnp.float32
BF16 = jnp.bfloat16


def _pick(n, prefs):
    for t in prefs:
        if n % t == 0:
            return t
    raise ValueError(f"no tile in {prefs} divides {n}")


def _rms_scale(x):
    return lax.rsqrt(jnp.mean(x * x, axis=-1, keepdims=True) + EPS)


def _qkv_kernel(x_ref, g_ref, w_ref, cos_ref, sina_ref, sinb_ref, o_ref, a_ref, *,
                blocks_per_region, sb_scale, df_scale):
    j = pl.program_id(1)

    @pl.when(j == 0)
    def _():
        x = x_ref[...]
        a_ref[...] = (x * _rms_scale(x) * g_ref[...]).astype(BF16)

    acc = jnp.dot(a_ref[...], w_ref[...], preferred_element_type=F32)
    region = j // blocks_per_region
    tn = acc.shape[1]

    def rope_store(scale):
        for c in range(tn // LANES):
            blk = acc[:, c * LANES:(c + 1) * LANES]
            r = (blk * cos_ref[...]
                 + pltpu.roll(blk, LANES - ROT_DIM // 2, 1) * sina_ref[...]
                 + pltpu.roll(blk, ROT_DIM // 2, 1) * sinb_ref[...])
            o_ref[:, c * LANES:(c + 1) * LANES] = (r * scale).astype(o_ref.dtype)

    @pl.when(region == 0)
    def _():
        o_ref[...] = (acc * sb_scale).astype(o_ref.dtype)

    @pl.when(region == 3)
    def _():
        rope_store(df_scale)

    @pl.when(region == 4)
    def _():
        rope_store(1.0)

    @pl.when((region == 1) | (region == 2) | (region == 5))
    def _():
        o_ref[...] = acc.astype(o_ref.dtype)


def _rope_tables(seq):
    half = ROT_DIM // 2
    inv = ROPE_THETA ** (-jnp.arange(0, ROT_DIM, 2, dtype=F32) / ROT_DIM)
    ang = jnp.arange(seq, dtype=F32)[:, None] * inv[None, :]
    cos, sin = jnp.cos(ang), jnp.sin(ang)
    comp = jnp.concatenate([cos, cos, jnp.ones((seq, DIFF_QK_DIM - ROT_DIM), F32)], axis=1)
    zeros_h = jnp.zeros((seq, half), F32)
    zeros_r = jnp.zeros((seq, DIFF_QK_DIM - ROT_DIM), F32)
    sina = jnp.concatenate([-sin, zeros_h, zeros_r], axis=1)
    sinb = jnp.concatenate([zeros_h, sin, zeros_r], axis=1)
    rep = LANES // DIFF_QK_DIM
    return jnp.tile(comp, (1, rep)), jnp.tile(sina, (1, rep)), jnp.tile(sinb, (1, rep))


def _qkv_proj(x2, g, w_bf16, tables, seq):
    n, d = x2.shape
    cols = w_bf16.shape[1]
    region = cols // 6
    tm = _pick(seq, (1024, 512, 256, 128))
    tn = _pick(region, (512, 256, 128))
    pos_blocks = seq // tm
    kern = functools.partial(_qkv_kernel, blocks_per_region=region // tn,
                             sb_scale=LOG2E / math.sqrt(HEAD_DIM), df_scale=LOG2E / math.sqrt(DIFF_QK_DIM))
    tab_spec = pl.BlockSpec((tm, LANES), lambda i, j: (i % pos_blocks, 0))
    return pl.pallas_call(
        kern,
        grid=(n // tm, cols // tn),
        in_specs=[pl.BlockSpec((tm, d), lambda i, j: (i, 0)),
                  pl.BlockSpec((1, d), lambda i, j: (0, 0)),
                  pl.BlockSpec((d, tn), lambda i, j: (0, j)),
                  tab_spec, tab_spec, tab_spec],
        out_specs=pl.BlockSpec((tm, tn), lambda i, j: (i, j)),
        out_shape=jax.ShapeDtypeStruct((n, cols), BF16),
        scratch_shapes=[pltpu.VMEM((tm, d), BF16)],
        compiler_params=pltpu.CompilerParams(dimension_semantics=("parallel", "arbitrary"),
                                             vmem_limit_bytes=V7X_VMEM_LIMIT),
        name="qkv_proj",
    )(x2, g, w_bf16, *tables)


def _build_v_transposed(v_ref, vt_ref, heads):
    bk = vt_ref.shape[3]

    def body(c, carry):
        off = pl.multiple_of(c * bk, bk)
        for g in range(heads):
            blk = v_ref[0, pl.ds(off, bk), g * HEAD_DIM:(g + 1) * HEAD_DIM]
            vt_ref[g, c] = blk.astype(F32).T.astype(BF16)
        return carry

    lax.fori_loop(0, vt_ref.shape[1], body, 0)


def _sweep_key_tiles(i, tiles_per_block, bk, tile, *, latest_first):
    first_local = i * tiles_per_block

    def local(rr, c):
        r = tiles_per_block - 1 - rr if latest_first else rr
        tile(first_local + r, r * bk)
        return c

    def earlier(jj, c):
        tile(first_local - 1 - jj if latest_first else jj, None)
        return c

    lax.fori_loop(0, tiles_per_block, local, 0)
    lax.fori_loop(0, first_local, earlier, 0)


def _sb_attn_kernel(q_ref, k_ref, v_ref, tri_ref, g_ref, o_ref, vt_ref, acc_ref, carry_ref, *, bq, bk, heads):
    i = pl.program_id(2)

    @pl.when(i == 0)
    def _():
        _build_v_transposed(v_ref, vt_ref, heads)

    q_t = [q_ref[0, :, g * HEAD_DIM:(g + 1) * HEAD_DIM].astype(F32).T.astype(BF16) for g in range(heads)]
    tri = tri_ref[...]

    def tile(j, key_base):
        off = pl.multiple_of(j * bk, bk)
        if key_base is not None:
            key = lax.broadcasted_iota(jnp.int32, (bk, bq), 0) + key_base
            qry = lax.broadcasted_iota(jnp.int32, (bk, bq), 1)
            mask = key < qry
        hs = range(heads)
        zs = [jnp.dot(k_ref[0, pl.ds(off, bk), g * HEAD_DIM:(g + 1) * HEAD_DIM], q_t[g],
                      preferred_element_type=F32) for g in hs]
        sps = []
        for z in zs:
            neg_log_1mb = jnp.where(z > 32.0, z, jnp.log2(1.0 + jnp.exp2(z)))
            if key_base is not None:
                neg_log_1mb = jnp.where(mask, neg_log_1mb, 0.0)
            sps.append(neg_log_1mb)
        sufs = [jnp.dot(tri, sp.astype(BF16), preferred_element_type=F32) for sp in sps]
        for g in hs:
            a = jnp.exp2(zs[g] - sps[g] - sufs[g])
            if key_base is not None:
                a = jnp.where(mask, a, 0.0)
            pv = jnp.dot(vt_ref[g, j], a.astype(BF16), preferred_element_type=F32)
            acc_ref[g] += pv * jnp.exp2(-carry_ref[g])
        for g in hs:
            carry_ref[g] += sufs[g][0:1, :] + sps[g][0:1, :]

    acc_ref[...] = jnp.zeros_like(acc_ref)
    carry_ref[...] = jnp.zeros_like(carry_ref)
    _sweep_key_tiles(i, bq // bk, bk, tile, latest_first=True)
    for g in range(heads):
        o = acc_ref[g].T
        o_ref[0, :, g * HEAD_DIM:(g + 1) * HEAD_DIM] = (o * _rms_scale(o) * g_ref[...]).astype(o_ref.dtype)


def _attn_specs(b, s, n_heads, col0, bq, heads):
    width = heads * HEAD_DIM
    c0, groups = col0 // heads, n_heads // heads
    return ([pl.BlockSpec((1, bq, width), lambda bi, h, i: (bi, i, c0 + h)),
             pl.BlockSpec((1, s, width), lambda bi, h, i: (bi, 0, c0 + groups + h)),
             pl.BlockSpec((1, s, width), lambda bi, h, i: (bi, 0, c0 + 2 * groups + h))],
            pl.BlockSpec((1, bq, width), lambda bi, h, i: (bi, i, h)),
            (b, groups, s // bq))


def _sb_attention(qkv3, tri, g, *, n_heads, col0, bq, bk, heads):
    b, s, _ = qkv3.shape
    qkv_specs, out_spec, grid = _attn_specs(b, s, n_heads, col0, bq, heads)
    return pl.pallas_call(
        functools.partial(_sb_attn_kernel, bq=bq, bk=bk, heads=heads),
        grid=grid,
        in_specs=qkv_specs + [pl.BlockSpec((bk, bk), lambda bi, h, i: (0, 0)),
                              pl.BlockSpec((1, HEAD_DIM), lambda bi, h, i: (0, 0))],
        out_specs=out_spec,
        out_shape=jax.ShapeDtypeStruct((b, s, n_heads * HEAD_DIM), BF16),
        scratch_shapes=[pltpu.VMEM((heads, s // bk, HEAD_DIM, bk), BF16),
                        pltpu.VMEM((heads, HEAD_DIM, bq), F32), pltpu.VMEM((heads, 1, bq), F32)],
        compiler_params=pltpu.CompilerParams(dimension_semantics=("parallel", "parallel", "arbitrary"),
                                             vmem_limit_bytes=V7X_VMEM_LIMIT),
        name="sb_attention",
    )(qkv3, qkv3, qkv3, tri, g)


def _diff_attn_kernel(q_ref, k_ref, v_ref, lq1_ref, lk1_ref, lq2_ref, lk2_ref, g_ref, o_ref,
                      vt_ref, acc_ref, m_ref, l_ref, *, bq, bk, heads, lam_init):
    i = pl.program_id(2)

    @pl.when(i == 0)
    def _():
        _build_v_transposed(v_ref, vt_ref, heads)

    dim = lax.broadcasted_iota(jnp.int32, (HEAD_DIM, bq), 0)
    q_t = []
    for g in range(heads):
        qt = q_ref[0, :, g * HEAD_DIM:(g + 1) * HEAD_DIM].astype(F32).T
        q_t.append((jnp.where(dim < DIFF_QK_DIM, qt, 0.0).astype(BF16),
                    jnp.where(dim >= DIFF_QK_DIM, qt, 0.0).astype(BF16)))

    def tile(j, key_base):
        off = pl.multiple_of(j * bk, bk)
        if key_base is not None:
            key = lax.broadcasted_iota(jnp.int32, (bk, bq), 0) + key_base
            qry = lax.broadcasted_iota(jnp.int32, (bk, bq), 1)
            mask = (key // CHUNK) <= (qry // CHUNK)
        chains = [(g, c) for g in range(heads) for c in range(2)]
        ss = []
        for g, c in chains:
            k = k_ref[0, pl.ds(off, bk), g * HEAD_DIM:(g + 1) * HEAD_DIM]
            s = jnp.dot(k, q_t[g][c], preferred_element_type=F32)
            ss.append(jnp.where(mask, s, -1e30) if key_base is not None else s)
        ps, alphas = [], []
        for (g, c), s in zip(chains, ss):
            m_prev = m_ref[g, c]
            m_new = jnp.maximum(m_prev, jnp.max(s, axis=0, keepdims=True))
            alpha = jnp.exp2(m_prev - m_new)
            p = jnp.exp2(s - m_new)
            l_ref[g, c] = alpha * l_ref[g, c] + jnp.sum(p, axis=0, keepdims=True)
            m_ref[g, c] = m_new
            ps.append(p.astype(BF16))
            alphas.append(alpha)
        pvs = [jnp.dot(vt_ref[g, j], p, preferred_element_type=F32) for (g, c), p in zip(chains, ps)]
        for (g, c), alpha, pv in zip(chains, alphas, pvs):
            acc_ref[g, c] = alpha * acc_ref[g, c] + pv

    acc_ref[...] = jnp.zeros_like(acc_ref)
    m_ref[...] = jnp.full_like(m_ref, -1e30)
    l_ref[...] = jnp.zeros_like(l_ref)
    _sweep_key_tiles(i, bq // bk, bk, tile, latest_first=False)
    lam = (jnp.exp(jnp.sum(lq1_ref[...] * lk1_ref[...], axis=-1, keepdims=True))
           - jnp.exp(jnp.sum(lq2_ref[...] * lk2_ref[...], axis=-1, keepdims=True)) + lam_init)
    for g in range(heads):
        o = (acc_ref[g, 0] / l_ref[g, 0] - lam * (acc_ref[g, 1] / l_ref[g, 1])).T
        o_ref[0, :, g * HEAD_DIM:(g + 1) * HEAD_DIM] = (
            o * _rms_scale(o) * g_ref[...] * (1.0 - lam_init)).astype(o_ref.dtype)


def _diff_attention(qkv3, lams, g, *, n_heads, col0, bq, bk, heads, lam_init):
    b, s, _ = qkv3.shape
    qkv_specs, out_spec, grid = _attn_specs(b, s, n_heads, col0, bq, heads)
    vec = lambda w: pl.BlockSpec((1, w), lambda bi, h, i: (0, 0))
    return pl.pallas_call(
        functools.partial(_diff_attn_kernel, bq=bq, bk=bk, heads=heads, lam_init=lam_init),
        grid=grid,
        in_specs=qkv_specs + [vec(DIFF_QK_DIM), vec(DIFF_QK_DIM), vec(DIFF_QK_DIM), vec(DIFF_QK_DIM), vec(HEAD_DIM)],
        out_specs=out_spec,
        out_shape=jax.ShapeDtypeStruct((b, s, n_heads * HEAD_DIM), BF16),
        scratch_shapes=[pltpu.VMEM((heads, s // bk, HEAD_DIM, bk), BF16),
                        pltpu.VMEM((heads, 2, HEAD_DIM, bq), F32),
                        pltpu.VMEM((heads, 2, 1, bq), F32), pltpu.VMEM((heads, 2, 1, bq), F32)],
        compiler_params=pltpu.CompilerParams(dimension_semantics=("parallel", "parallel", "arbitrary"),
                                             vmem_limit_bytes=V7X_VMEM_LIMIT),
        name="diff_attention",
    )(qkv3, qkv3, qkv3, *lams, g)


def _outproj_kernel(x_ref, osb_ref, odf_ref, w_ref, g_ref, wr_ref, h_ref, m_ref, lg_ref, *, sb_width):
    h = (x_ref[...]
         + jnp.dot(osb_ref[...], w_ref[:sb_width, :], preferred_element_type=F32)
         + jnp.dot(odf_ref[...], w_ref[sb_width:, :], preferred_element_type=F32))
    h_ref[...] = h
    m = h * _rms_scale(h) * g_ref[...]
    m_ref[...] = m
    lg_ref[...] = jnp.dot(m, wr_ref[...], preferred_element_type=F32, precision=lax.Precision.HIGHEST)


def _outproj_router(x2, o_sb, o_df, w_out_bf16, g_moe, w_router):
    n, d = x2.shape
    sbw, dfw = o_sb.shape[1], o_df.shape[1]
    tm = _pick(n, (512, 256, 128))
    row = lambda w: pl.BlockSpec((tm, w), lambda i: (i, 0))
    full = lambda a: pl.BlockSpec(a.shape, lambda i: (0, 0))
    return pl.pallas_call(
        functools.partial(_outproj_kernel, sb_width=sbw),
        grid=(n // tm,),
        in_specs=[row(d), row(sbw), row(dfw), full(w_out_bf16), full(g_moe), full(w_router)],
        out_specs=[row(d), row(d), row(LANES)],
        out_shape=[jax.ShapeDtypeStruct((n, d), F32), jax.ShapeDtypeStruct((n, d), F32),
                   jax.ShapeDtypeStruct((n, LANES), F32)],
        compiler_params=pltpu.CompilerParams(dimension_semantics=("parallel",),
                                             vmem_limit_bytes=V7X_VMEM_LIMIT),
        name="outproj_router",
    )(x2, o_sb, o_df, w_out_bf16, g_moe, w_router)


def _gather_rows(idx_ref, src_hbm, dst_ref, sem, count):
    def issue(r, c):
        pltpu.make_async_copy(src_hbm.at[pl.ds(idx_ref[0, 0, r], 1), :], dst_ref.at[pl.ds(r, 1), :], sem).start()
        return c

    lax.fori_loop(0, count, issue, 0)
    pltpu.make_async_copy(src_hbm.at[pl.ds(0, count), :], dst_ref, sem).wait()


def _expert_kernel(be_ref, rows_ref, m_hbm, wg_ref, wu_ref, wd_ref, y_ref, x_buf, sem):
    del be_ref
    _gather_rows(rows_ref, m_hbm, x_buf, sem, x_buf.shape[0])
    xb = x_buf[...].astype(BF16)
    g = jnp.dot(xb, wg_ref[0], preferred_element_type=F32)
    u = jnp.dot(xb, wu_ref[0], preferred_element_type=F32)
    hmid = g * (1.0 / (1.0 + jnp.exp(-g))) * u
    y_ref[...] = jnp.dot(hmid.astype(BF16), wd_ref[0], preferred_element_type=F32)


def _expert_ffn(m, rows3, blk_e, wg, wu, wd):
    n, d = m.shape
    nb, _, blk = rows3.shape
    de = wg.shape[2]
    grid_spec = pltpu.PrefetchScalarGridSpec(
        num_scalar_prefetch=1,
        grid=(nb,),
        in_specs=[pl.BlockSpec((1, 1, blk), lambda i, be: (i, 0, 0), memory_space=pltpu.SMEM),
                  pl.BlockSpec(memory_space=pl.ANY),
                  pl.BlockSpec((1, d, de), lambda i, be: (be[i], 0, 0)),
                  pl.BlockSpec((1, d, de), lambda i, be: (be[i], 0, 0)),
                  pl.BlockSpec((1, de, d), lambda i, be: (be[i], 0, 0))],
        out_specs=pl.BlockSpec((blk, d), lambda i, be: (i, 0)),
        scratch_shapes=[pltpu.VMEM((blk, d), F32), pltpu.SemaphoreType.DMA(())],
    )
    return pl.pallas_call(
        _expert_kernel,
        grid_spec=grid_spec,
        out_shape=jax.ShapeDtypeStruct((nb * blk, d), F32),
        compiler_params=pltpu.CompilerParams(dimension_semantics=("arbitrary",),
                                             vmem_limit_bytes=V7X_VMEM_LIMIT),
        name="expert_ffn",
    )(blk_e, rows3, m, wg, wu, wd)


def _combine_kernel(pa_ref, pb_ref, ys_hbm, h_ref, gates_ref, p_ref, gple_ref, wple_ref, wgate_ref, gfin_ref,
                    o_ref, ya_buf, yb_buf, sem_a, sem_b, *, final):
    tm = h_ref.shape[0]
    _gather_rows(pa_ref, ys_hbm, ya_buf, sem_a, tm)
    _gather_rows(pb_ref, ys_hbm, yb_buf, sem_b, tm)
    gates = gates_ref[...]
    h = h_ref[...] + gates[:, 0:1] * ya_buf[...] + gates[:, 1:2] * yb_buf[...]
    e = (h * _rms_scale(h) * gple_ref[...]).astype(BF16)
    z = jnp.dot(e, wgate_ref[...], preferred_element_type=F32)
    pe = jnp.dot(p_ref[...].astype(BF16), wple_ref[...], preferred_element_type=F32)
    h = h + pe * (1.0 / (1.0 + jnp.exp(-z)))
    if final:
        h = h * _rms_scale(h) * gfin_ref[...]
    o_ref[...] = h


def _combine_ple(ys, pos_a, pos_b, h1, gates, p2, g_ple, w_ple, w_ple_gate, g_final, *, final):
    n, d = h1.shape
    nt, _, tm = pos_a.shape
    row = lambda w: pl.BlockSpec((tm, w), lambda i: (i, 0))
    full = lambda a: pl.BlockSpec(a.shape, lambda i: (0, 0))
    idx = pl.BlockSpec((1, 1, tm), lambda i: (i, 0, 0), memory_space=pltpu.SMEM)
    return pl.pallas_call(
        functools.partial(_combine_kernel, final=final),
        grid=(nt,),
        in_specs=[idx, idx, pl.BlockSpec(memory_space=pl.ANY), row(d), row(gates.shape[1]), row(p2.shape[1]),
                  full(g_ple), full(w_ple), full(w_ple_gate), full(g_final)],
        out_specs=row(d),
        out_shape=jax.ShapeDtypeStruct((n, d), F32),
        scratch_shapes=[pltpu.VMEM((tm, d), F32), pltpu.VMEM((tm, d), F32),
                        pltpu.SemaphoreType.DMA(()), pltpu.SemaphoreType.DMA(())],
        compiler_params=pltpu.CompilerParams(dimension_semantics=("arbitrary",),
                                             vmem_limit_bytes=V7X_VMEM_LIMIT),
        name="combine_ple",
    )(pos_a, pos_b, ys, h1, gates, p2, g_ple, w_ple, w_ple_gate, g_final)


def _route(logits, n_groups, n_experts, blk):
    n = logits.shape[0]
    epg = n_experts // n_groups
    p_grp = jax.nn.softmax(logits[:, :n_groups], axis=-1)
    grp = jnp.argmax(p_grp, axis=-1)
    p_sel = jnp.take_along_axis(p_grp, grp[:, None], axis=1)[:, 0]
    e_logits = logits[:, n_groups:n_groups + n_experts].reshape(n, n_groups, epg)
    in_grp = jnp.take_along_axis(e_logits, grp[:, None, None], axis=1)[:, 0]
    top_v, top_i = lax.top_k(in_grp, TOP_K_INNER)
    gates = jax.nn.softmax(top_v, axis=-1) * p_sel[:, None]
    expert = (grp[:, None] * epg + top_i).astype(jnp.int32)

    nk = n * TOP_K_INNER
    flat_e = expert.reshape(nk)
    onehot = (flat_e[:, None] == jnp.arange(n_experts, dtype=jnp.int32)[None, :]).astype(jnp.int32)
    csum = jnp.cumsum(onehot, axis=0)
    rank = jnp.take_along_axis(csum, flat_e[:, None], axis=1)[:, 0] - 1
    counts = csum[-1]
    padded = ((counts + blk - 1) // blk) * blk
    pends = jnp.cumsum(padded)
    pstarts = pends - padded
    dest = (pstarts[flat_e] + rank).astype(jnp.int32)
    total = nk + n_experts * blk
    nb = total // blk
    flat_tok = jnp.repeat(jnp.arange(n, dtype=jnp.int32), TOP_K_INNER)
    rows = jnp.zeros((total,), jnp.int32).at[dest].set(flat_tok)
    blk_e = jnp.clip(jnp.searchsorted(pends, jnp.arange(nb, dtype=jnp.int32) * blk, side='right'),
                     0, n_experts - 1).astype(jnp.int32)
    return gates, dest.reshape(n, TOP_K_INNER), rows.reshape(nb, 1, blk), blk_e


def kernel(x, p, g_attn, w_in, g_sb, g_diff, lam_q1, lam_k1, lam_q2, lam_k2, w_out, g_moe, w_router_group,
           w_router_expert, w_gate, w_up, w_down, g_ple, w_ple, w_ple_gate, g_final):
    b, s, d = x.shape
    n = b * s
    depth = p.shape[0]
    sb_width = d // 2
    n_heads = sb_width // HEAD_DIM
    n_groups = w_router_group.shape[2]
    n_experts = w_router_expert.shape[2]
    assert n_groups + n_experts <= LANES and w_in.shape[2] == 6 * sb_width
    bq = _pick(s, (512, 256, 128))
    bk = _pick(bq, (256, 128))
    moe_blk = 128
    attn_heads = _pick(n_heads, (4, 2, 1))
    tm_c = _pick(n, (256, 128))

    tables = _rope_tables(s)
    tri = (jnp.arange(bk)[None, :] > jnp.arange(bk)[:, None]).astype(BF16)
    row2 = lambda a: a.reshape(1, -1)
    blocks = sb_width // HEAD_DIM

    h = x.reshape(n, d)
    for i in range(depth):
        qkv = _qkv_proj(h, row2(g_attn[i]), w_in[i].astype(BF16), tables, s)
        qkv3 = qkv.reshape(b, s, -1)
        o_sb = _sb_attention(qkv3, tri, row2(g_sb[i]), n_heads=n_heads, col0=0, bq=bq, bk=bk,
                             heads=attn_heads)
        lam_init = 0.8 - 0.6 * math.exp(-0.3 * i)
        lams = (row2(lam_q1[i]), row2(lam_k1[i]), row2(lam_q2[i]), row2(lam_k2[i]))
        o_df = _diff_attention(qkv3, lams, row2(g_diff[i]), n_heads=n_heads, col0=3 * blocks, bq=bq,
                               bk=bk, heads=attn_heads, lam_init=lam_init)
        w_router = jnp.concatenate(
            [w_router_group[i], w_router_expert[i], jnp.zeros((d, LANES - n_groups - n_experts), F32)], axis=1)
        h1, m, logits = _outproj_router(h, o_sb.reshape(n, -1), o_df.reshape(n, -1), w_out[i].astype(BF16),
                                        row2(g_moe[i]), w_router)
        gates, pos, rows3, blk_e = _route(logits, n_groups, n_experts, moe_blk)
        ys = _expert_ffn(m, rows3, blk_e, w_gate[i].astype(BF16), w_up[i].astype(BF16), w_down[i].astype(BF16))
        pos_a = pos[:, 0].reshape(n // tm_c, 1, tm_c)
        pos_b = pos[:, 1].reshape(n // tm_c, 1, tm_c)
        h = _combine_ple(ys, pos_a, pos_b, h1, gates, p[i].reshape(n, -1), row2(g_ple[i]), w_ple[i].astype(BF16),
                         w_ple_gate[i].astype(BF16), row2(g_final), final=(i == depth - 1))
    return h.reshape(b, s, d)
```

```python
import functools
import math

import jax
import jax.numpy as jnp
from jax import lax
from jax.experimental import pallas as pl
from jax.experimental.pallas import tpu as pltpu

EPS = 1e-6
HEAD_DIM = 128
DIFF_QK_DIM = 64
CHUNK = 64
ROPE_THETA = 500000.0
ROT_DIM = DIFF_QK_DIM // 4
TOP_K_INNER = 2
LOG2E = 1.4426950408889634
LANES = 128
V7X_VMEM_LIMIT = 56 * 1024 * 1024

F32 = jnp.float32
BF16 = jnp.bfloat16


def _pick(n, prefs):
    for t in prefs:
        if n % t == 0:
            return t
    raise ValueError(f"no tile in {prefs} divides {n}")


def _rms_scale(x):
    return lax.rsqrt(jnp.mean(x * x, axis=-1, keepdims=True) + EPS)


def _qkv_kernel(x_ref, g_ref, w_ref, cos_ref, sina_ref, sinb_ref, o_ref, a_ref, *,
                blocks_per_region, sb_scale, df_scale):
    j = pl.program_id(1)

    @pl.when(j == 0)
    def _():
        x = x_ref[...]
        a_ref[...] = (x * _rms_scale(x) * g_ref[...]).astype(BF16)

    acc = jnp.dot(a_ref[...], w_ref[...], preferred_element_type=F32)
    region = j // blocks_per_region
    tn = acc.shape[1]

    def rope_store(scale):
        for c in range(tn // LANES):
            blk = acc[:, c * LANES:(c + 1) * LANES]
            r = (blk * cos_ref[...]
                 + pltpu.roll(blk, LANES - ROT_DIM // 2, 1) * sina_ref[...]
                 + pltpu.roll(blk, ROT_DIM // 2, 1) * sinb_ref[...])
            o_ref[:, c * LANES:(c + 1) * LANES] = (r * scale).astype(o_ref.dtype)

    @pl.when(region == 0)
    def _():
        o_ref[...] = (acc * sb_scale).astype(o_ref.dtype)

    @pl.when(region == 3)
    def _():
        rope_store(df_scale)

    @pl.when(region == 4)
    def _():
        rope_store(1.0)

    @pl.when((region == 1) | (region == 2) | (region == 5))
    def _():
        o_ref[...] = acc.astype(o_ref.dtype)


def _rope_tables(seq):
    half = ROT_DIM // 2
    inv = ROPE_THETA ** (-jnp.arange(0, ROT_DIM, 2, dtype=F32) / ROT_DIM)
    ang = jnp.arange(seq, dtype=F32)[:, None] * inv[None, :]
    cos, sin = jnp.cos(ang), jnp.sin(ang)
    comp = jnp.concatenate([cos, cos, jnp.ones((seq, DIFF_QK_DIM - ROT_DIM), F32)], axis=1)
    zeros_h = jnp.zeros((seq, half), F32)
    zeros_r = jnp.zeros((seq, DIFF_QK_DIM - ROT_DIM), F32)
    sina = jnp.concatenate([-sin, zeros_h, zeros_r], axis=1)
    sinb = jnp.concatenate([zeros_h, sin, zeros_r], axis=1)
    rep = LANES // DIFF_QK_DIM
    return jnp.tile(comp, (1, rep)), jnp.tile(sina, (1, rep)), jnp.tile(sinb, (1, rep))


def _qkv_proj(x2, g, w_bf16, tables, seq):
    n, d = x2.shape
    cols = w_bf16.shape[1]
    region = cols // 6
    tm = _pick(seq, (1024, 512, 256, 128))
    tn = _pick(region, (512, 256, 128))
    pos_blocks = seq // tm
    kern = functools.partial(_qkv_kernel, blocks_per_region=region // tn,
                             sb_scale=LOG2E / math.sqrt(HEAD_DIM), df_scale=LOG2E / math.sqrt(DIFF_QK_DIM))
    tab_spec = pl.BlockSpec((tm, LANES), lambda i, j: (i % pos_blocks, 0))
    return pl.pallas_call(
        kern,
        grid=(n // tm, cols // tn),
        in_specs=[pl.BlockSpec((tm, d), lambda i, j: (i, 0)),
                  pl.BlockSpec((1, d), lambda i, j: (0, 0)),
                  pl.BlockSpec((d, tn), lambda i, j: (0, j)),
                  tab_spec, tab_spec, tab_spec],
        out_specs=pl.BlockSpec((tm, tn), lambda i, j: (i, j)),
        out_shape=jax.ShapeDtypeStruct((n, cols), BF16),
        scratch_shapes=[pltpu.VMEM((tm, d), BF16)],
        compiler_params=pltpu.CompilerParams(dimension_semantics=("parallel", "arbitrary"),
                                             vmem_limit_bytes=V7X_VMEM_LIMIT),
        name="qkv_proj",
    )(x2, g, w_bf16, *tables)


def _build_v_transposed(v_ref, vt_ref, heads):
    bk = vt_ref.shape[3]

    def body(c, carry):
        off = pl.multiple_of(c * bk, bk)
        for g in range(heads):
            blk = v_ref[0, pl.ds(off, bk), g * HEAD_DIM:(g + 1) * HEAD_DIM]
            vt_ref[g, c] = blk.astype(F32).T.astype(BF16)
        return carry

    lax.fori_loop(0, vt_ref.shape[1], body, 0)


def _sweep_key_tiles(i, tiles_per_block, bk, tile, *, descending):
    first_local = i * tiles_per_block

    def local(s, c):
        r = tiles_per_block - 1 - s if descending else s
        tile(first_local + r, r * bk)
        return c

    def earlier(s, c):
        tile(first_local - 1 - s if descending else s, None)
        return c

    if descending:
        lax.fori_loop(0, tiles_per_block, local, 0)
        lax.fori_loop(0, first_local, earlier, 0)
    else:
        lax.fori_loop(0, first_local, earlier, 0)
        lax.fori_loop(0, tiles_per_block, local, 0)


def _sb_attn_kernel(q_ref, k_ref, v_ref, tri_ref, g_ref, o_ref, vt_ref, acc_ref, carry_ref, *, bq, bk, heads):
    i = pl.program_id(2)

    @pl.when(i == 0)
    def _():
        _build_v_transposed(v_ref, vt_ref, heads)

    q_t = [q_ref[0, :, g * HEAD_DIM:(g + 1) * HEAD_DIM].astype(F32).T.astype(BF16) for g in range(heads)]
    tri = tri_ref[...]
    hs = range(heads)

    def tile(j, key_base):
        off = pl.multiple_of(j * bk, bk)
        if key_base is not None:
            key = lax.broadcasted_iota(jnp.int32, (bk, bq), 0) + key_base
            qry = lax.broadcasted_iota(jnp.int32, (bk, bq), 1)
            mask = key < qry
        zs = [jnp.dot(k_ref[0, pl.ds(off, bk), g * HEAD_DIM:(g + 1) * HEAD_DIM], q_t[g],
                      preferred_element_type=F32) for g in hs]
        sps, log_betas, sp_row0 = [], [], []
        for z in zs:
            z = jnp.minimum(z, 126.0)
            neg_log_1mb = jnp.log2(1.0 + jnp.exp2(z))
            log_beta = z - neg_log_1mb
            if key_base is not None:
                neg_log_1mb = jnp.where(mask, neg_log_1mb, 0.0)
            sps.append(neg_log_1mb.astype(BF16))
            log_betas.append(log_beta)
            sp_row0.append(neg_log_1mb[0:1, :])
        sufs = [jnp.dot(tri, sp, preferred_element_type=F32) for sp in sps]
        for g in hs:
            a = jnp.exp2(log_betas[g] - sufs[g])
            if key_base is not None:
                a = jnp.where(mask, a, 0.0)
            pv = jnp.dot(vt_ref[g, j], a.astype(BF16), preferred_element_type=F32)
            acc_ref[g] += pv * jnp.exp2(-carry_ref[g])
        for g in hs:
            carry_ref[g] += sufs[g][0:1, :] + sp_row0[g]

    acc_ref[...] = jnp.zeros_like(acc_ref)
    carry_ref[...] = jnp.zeros_like(carry_ref)
    _sweep_key_tiles(i, bq // bk, bk, tile, descending=True)
    for g in range(heads):
        o = acc_ref[g].T
        o_ref[0, :, g * HEAD_DIM:(g + 1) * HEAD_DIM] = (o * _rms_scale(o) * g_ref[...]).astype(o_ref.dtype)


def _attn_specs(b, s, n_heads, col0, bq, heads):
    width = heads * HEAD_DIM
    c0, groups = col0 // heads, n_heads // heads
    return ([pl.BlockSpec((1, bq, width), lambda bi, h, i: (bi, i, c0 + h)),
             pl.BlockSpec((1, s, width), lambda bi, h, i: (bi, 0, c0 + groups + h)),
             pl.BlockSpec((1, s, width), lambda bi, h, i: (bi, 0, c0 + 2 * groups + h))],
            pl.BlockSpec((1, bq, width), lambda bi, h, i: (bi, i, h)),
            (b, groups, s // bq))


def _sb_attention(qkv3, tri, g, *, n_heads, col0, bq, bk, heads):
    b, s, _ = qkv3.shape
    qkv_specs, out_spec, grid = _attn_specs(b, s, n_heads, col0, bq, heads)
    return pl.pallas_call(
        functools.partial(_sb_attn_kernel, bq=bq, bk=bk, heads=heads),
        grid=grid,
        in_specs=qkv_specs + [pl.BlockSpec((bk, bk), lambda bi, h, i: (0, 0)),
                              pl.BlockSpec((1, HEAD_DIM), lambda bi, h, i: (0, 0))],
        out_specs=out_spec,
        out_shape=jax.ShapeDtypeStruct((b, s, n_heads * HEAD_DIM), BF16),
        scratch_shapes=[pltpu.VMEM((heads, s // bk, HEAD_DIM, bk), BF16),
                        pltpu.VMEM((heads, HEAD_DIM, bq), F32), pltpu.VMEM((heads, 1, bq), F32)],
        compiler_params=pltpu.CompilerParams(dimension_semantics=("parallel", "parallel", "arbitrary"),
                                             vmem_limit_bytes=V7X_VMEM_LIMIT),
        name="sb_attention",
    )(qkv3, qkv3, qkv3, tri, g)


def _diff_attn_kernel(q_ref, k_ref, v_ref, lq1_ref, lk1_ref, lq2_ref, lk2_ref, g_ref, o_ref,
                      vt_ref, acc_ref, m_ref, l_ref, *, bq, bk, heads, lam_init):
    i = pl.program_id(2)

    @pl.when(i == 0)
    def _():
        _build_v_transposed(v_ref, vt_ref, heads)

    dim = lax.broadcasted_iota(jnp.int32, (HEAD_DIM, bq), 0)
    q_t = []
    for g in range(heads):
        qt = q_ref[0, :, g * HEAD_DIM:(g + 1) * HEAD_DIM].astype(F32).T
        q_t.append((jnp.where(dim < DIFF_QK_DIM, qt, 0.0).astype(BF16),
                    jnp.where(dim >= DIFF_QK_DIM, qt, 0.0).astype(BF16)))

    chains = [(g, c) for g in range(heads) for c in range(2)]

    def tile(j, key_base):
        off = pl.multiple_of(j * bk, bk)
        if key_base is not None:
            key = lax.broadcasted_iota(jnp.int32, (bk, bq), 0) + key_base
            qry = lax.broadcasted_iota(jnp.int32, (bk, bq), 1)
            mask = (key // CHUNK) <= (qry // CHUNK)
        ss = []
        for g, c in chains:
            s = jnp.dot(k_ref[0, pl.ds(off, bk), g * HEAD_DIM:(g + 1) * HEAD_DIM], q_t[g][c],
                        preferred_element_type=F32)
            ss.append(jnp.where(mask, s, -1e30) if key_base is not None else s)
        ps, alphas = [], []
        for (g, c), s in zip(chains, ss):
            m_prev = m_ref[g, c]
            m_new = jnp.maximum(m_prev, jnp.max(s, axis=0, keepdims=True))
            alpha = jnp.exp2(m_prev - m_new)
            p = jnp.exp2(s - m_new)
            l_ref[g, c] = alpha * l_ref[g, c] + jnp.sum(p, axis=0, keepdims=True)
            m_ref[g, c] = m_new
            ps.append(p.astype(BF16))
            alphas.append(alpha)
        pvs = [jnp.dot(vt_ref[g, j], p, preferred_element_type=F32) for (g, c), p in zip(chains, ps)]
        for (g, c), alpha, pv in zip(chains, alphas, pvs):
            acc_ref[g, c] = alpha * acc_ref[g, c] + pv

    acc_ref[...] = jnp.zeros_like(acc_ref)
    m_ref[...] = jnp.full_like(m_ref, -1e30)
    l_ref[...] = jnp.zeros_like(l_ref)
    _sweep_key_tiles(i, bq // bk, bk, tile, descending=False)
    lam = (jnp.exp(jnp.sum(lq1_ref[...] * lk1_ref[...], axis=-1, keepdims=True))
           - jnp.exp(jnp.sum(lq2_ref[...] * lk2_ref[...], axis=-1, keepdims=True)) + lam_init)
    for g in range(heads):
        o = (acc_ref[g, 0] / l_ref[g, 0] - lam * (acc_ref[g, 1] / l_ref[g, 1])).T
        o_ref[0, :, g * HEAD_DIM:(g + 1) * HEAD_DIM] = (
            o * _rms_scale(o) * g_ref[...] * (1.0 - lam_init)).astype(o_ref.dtype)


def _diff_attention(qkv3, lams, g, *, n_heads, col0, bq, bk, heads, lam_init):
    b, s, _ = qkv3.shape
    qkv_specs, out_spec, grid = _attn_specs(b, s, n_heads, col0, bq, heads)
    vec = lambda w: pl.BlockSpec((1, w), lambda bi, h, i: (0, 0))
    return pl.pallas_call(
        functools.partial(_diff_attn_kernel, bq=bq, bk=bk, heads=heads, lam_init=lam_init),
        grid=grid,
        in_specs=qkv_specs + [vec(DIFF_QK_DIM), vec(DIFF_QK_DIM), vec(DIFF_QK_DIM), vec(DIFF_QK_DIM), vec(HEAD_DIM)],
        out_specs=out_spec,
        out_shape=jax.ShapeDtypeStruct((b, s, n_heads * HEAD_DIM), BF16),
        scratch_shapes=[pltpu.VMEM((heads, s // bk, HEAD_DIM, bk), BF16),
                        pltpu.VMEM((heads, 2, HEAD_DIM, bq), F32),
                        pltpu.VMEM((heads, 2, 1, bq), F32), pltpu.VMEM((heads, 2, 1, bq), F32)],
        compiler_params=pltpu.CompilerParams(dimension_semantics=("parallel", "parallel", "arbitrary"),
                                             vmem_limit_bytes=V7X_VMEM_LIMIT),
        name="diff_attention",
    )(qkv3, qkv3, qkv3, *lams, g)


def _outproj_kernel(x_ref, osb_ref, odf_ref, w_ref, g_ref, wr_ref, h_ref, m_ref, lg_ref, *, sb_width):
    h = (x_ref[...]
         + jnp.dot(osb_ref[...], w_ref[:sb_width, :], preferred_element_type=F32)
         + jnp.dot(odf_ref[...], w_ref[sb_width:, :], preferred_element_type=F32))
    h_ref[...] = h
    m = h * _rms_scale(h) * g_ref[...]
    m_ref[...] = m
    m_hi = m.astype(BF16)
    m_lo = (m - m_hi.astype(F32)).astype(BF16)
    hi_part = jnp.dot(m_hi, wr_ref[...], preferred_element_type=F32)
    lo_part = jnp.dot(m_lo, wr_ref[:, :LANES], preferred_element_type=F32)
    lg_ref[...] = hi_part[:, :LANES] + (hi_part[:, LANES:] + lo_part)


def _split_bf16(w):
    hi = w.astype(BF16)
    return jnp.concatenate([hi, (w - hi.astype(F32)).astype(BF16)], axis=1)


def _outproj_router(x2, o_sb, o_df, w_out_bf16, g_moe, w_router):
    n, d = x2.shape
    sbw, dfw = o_sb.shape[1], o_df.shape[1]
    tm = _pick(n, (512, 256, 128))
    row = lambda w: pl.BlockSpec((tm, w), lambda i: (i, 0))
    full = lambda a: pl.BlockSpec(a.shape, lambda i: (0, 0))
    return pl.pallas_call(
        functools.partial(_outproj_kernel, sb_width=sbw),
        grid=(n // tm,),
        in_specs=[row(d), row(sbw), row(dfw), full(w_out_bf16), full(g_moe), full(w_router)],
        out_specs=[row(d), row(d), row(LANES)],
        out_shape=[jax.ShapeDtypeStruct((n, d), F32), jax.ShapeDtypeStruct((n, d), F32),
                   jax.ShapeDtypeStruct((n, LANES), F32)],
        compiler_params=pltpu.CompilerParams(dimension_semantics=("parallel",),
                                             vmem_limit_bytes=V7X_VMEM_LIMIT),
        name="outproj_router",
    )(x2, o_sb, o_df, w_out_bf16, g_moe, w_router)


def _start_row_gather(idx_ref, src_hbm, dst_ref, sem):
    for r in range(dst_ref.shape[0]):
        pltpu.make_async_copy(src_hbm.at[pl.ds(idx_ref[0, 0, r], 1), :], dst_ref.at[pl.ds(r, 1), :], sem).start()


def _wait_row_gather(src_hbm, dst_ref, sem):
    pltpu.make_async_copy(src_hbm.at[pl.ds(0, dst_ref.shape[0]), :], dst_ref, sem).wait()


def _prefetched_gathers(gathers, consume, compute):
    i = pl.program_id(0)

    @pl.when(i == 0)
    def _():
        for idx_now, _, src, buf, sem in gathers:
            _start_row_gather(idx_now, src, buf, sem)

    for _, _, src, buf, sem in gathers:
        _wait_row_gather(src, buf, sem)
    consume()
    for _, idx_next, src, buf, sem in gathers:
        _start_row_gather(idx_next, src, buf, sem)
    compute()

    @pl.when(i == pl.num_programs(0) - 1)
    def _():
        for _, _, src, buf, sem in gathers:
            _wait_row_gather(src, buf, sem)


def _expert_kernel(be_ref, rows_ref, rows_next_ref, m_hbm, wg_ref, wu_ref, wd_ref, y_ref, x_buf, xb_ref, sem):
    del be_ref

    def consume():
        xb_ref[...] = x_buf[...].astype(BF16)

    def compute():
        xb = xb_ref[...]
        g = jnp.dot(xb, wg_ref[0], preferred_element_type=F32)
        u = jnp.dot(xb, wu_ref[0], preferred_element_type=F32)
        hmid = g * (1.0 / (1.0 + jnp.exp(-g))) * u
        y_ref[...] = jnp.dot(hmid.astype(BF16), wd_ref[0], preferred_element_type=F32)

    _prefetched_gathers([(rows_ref, rows_next_ref, m_hbm, x_buf, sem)], consume, compute)


def _expert_ffn(m, rows3, blk_e, wg, wu, wd):
    n, d = m.shape
    nb, _, blk = rows3.shape
    de = wg.shape[2]
    grid_spec = pltpu.PrefetchScalarGridSpec(
        num_scalar_prefetch=1,
        grid=(nb,),
        in_specs=[pl.BlockSpec((1, 1, blk), lambda i, be: (i, 0, 0), memory_space=pltpu.SMEM),
                  pl.BlockSpec((1, 1, blk), lambda i, be: (jnp.minimum(i + 1, nb - 1), 0, 0),
                               memory_space=pltpu.SMEM),
                  pl.BlockSpec(memory_space=pl.ANY),
                  pl.BlockSpec((1, d, de), lambda i, be: (be[i], 0, 0)),
                  pl.BlockSpec((1, d, de), lambda i, be: (be[i], 0, 0)),
                  pl.BlockSpec((1, de, d), lambda i, be: (be[i], 0, 0))],
        out_specs=pl.BlockSpec((blk, d), lambda i, be: (i, 0)),
        scratch_shapes=[pltpu.VMEM((blk, d), F32), pltpu.VMEM((blk, d), BF16), pltpu.SemaphoreType.DMA(())],
    )
    return pl.pallas_call(
        _expert_kernel,
        grid_spec=grid_spec,
        out_shape=jax.ShapeDtypeStruct((nb * blk, d), F32),
        compiler_params=pltpu.CompilerParams(dimension_semantics=("arbitrary",),
                                             vmem_limit_bytes=V7X_VMEM_LIMIT),
        name="expert_ffn",
    )(blk_e, rows3, rows3, m, wg, wu, wd)


def _combine_kernel(pa_ref, pa_next_ref, pb_ref, pb_next_ref, ys_hbm, h_ref, gates_ref, p_ref, gple_ref, wple_ref,
                    wgate_ref, gfin_ref, o_ref, ya_buf, yb_buf, h_buf, sem_a, sem_b, *, final):
    def consume():
        gates = gates_ref[...]
        h_buf[...] = h_ref[...] + gates[:, 0:1] * ya_buf[...] + gates[:, 1:2] * yb_buf[...]

    def compute():
        h = h_buf[...]
        e = (h * _rms_scale(h) * gple_ref[...]).astype(BF16)
        z = jnp.dot(e, wgate_ref[...], preferred_element_type=F32)
        pe = jnp.dot(p_ref[...].astype(BF16), wple_ref[...], preferred_element_type=F32)
        h = h + pe * (1.0 / (1.0 + jnp.exp(-z)))
        if final:
            h = h * _rms_scale(h) * gfin_ref[...]
        o_ref[...] = h

    _prefetched_gathers([(pa_ref, pa_next_ref, ys_hbm, ya_buf, sem_a),
                         (pb_ref, pb_next_ref, ys_hbm, yb_buf, sem_b)], consume, compute)


def _combine_ple(ys, pos_a, pos_b, h1, gates, p2, g_ple, w_ple, w_ple_gate, g_final, *, final):
    n, d = h1.shape
    nt, _, tm = pos_a.shape
    row = lambda w: pl.BlockSpec((tm, w), lambda i: (i, 0))
    full = lambda a: pl.BlockSpec(a.shape, lambda i: (0, 0))
    idx = pl.BlockSpec((1, 1, tm), lambda i: (i, 0, 0), memory_space=pltpu.SMEM)
    idx_next = pl.BlockSpec((1, 1, tm), lambda i: (jnp.minimum(i + 1, nt - 1), 0, 0), memory_space=pltpu.SMEM)
    return pl.pallas_call(
        functools.partial(_combine_kernel, final=final),
        grid=(nt,),
        in_specs=[idx, idx_next, idx, idx_next, pl.BlockSpec(memory_space=pl.ANY), row(d), row(gates.shape[1]),
                  row(p2.shape[1]), full(g_ple), full(w_ple), full(w_ple_gate), full(g_final)],
        out_specs=row(d),
        out_shape=jax.ShapeDtypeStruct((n, d), F32),
        scratch_shapes=[pltpu.VMEM((tm, d), F32), pltpu.VMEM((tm, d), F32), pltpu.VMEM((tm, d), F32),
                        pltpu.SemaphoreType.DMA(()), pltpu.SemaphoreType.DMA(())],
        compiler_params=pltpu.CompilerParams(dimension_semantics=("arbitrary",),
                                             vmem_limit_bytes=V7X_VMEM_LIMIT),
        name="combine_ple",
    )(pos_a, pos_a, pos_b, pos_b, ys, h1, gates, p2, g_ple, w_ple, w_ple_gate, g_final)


def _route(logits, n_groups, n_experts, blk):
    n = logits.shape[0]
    epg = n_experts // n_groups
    p_grp = jax.nn.softmax(logits[:, :n_groups], axis=-1)
    grp = jnp.argmax(p_grp, axis=-1)
    p_sel = jnp.take_along_axis(p_grp, grp[:, None], axis=1)[:, 0]
    e_logits = logits[:, n_groups:n_groups + n_experts].reshape(n, n_groups, epg)
    in_grp = jnp.take_along_axis(e_logits, grp[:, None, None], axis=1)[:, 0]
    top_v, top_i = lax.top_k(in_grp, TOP_K_INNER)
    gates = jax.nn.softmax(top_v, axis=-1) * p_sel[:, None]
    expert = (grp[:, None] * epg + top_i).astype(jnp.int32)

    nk = n * TOP_K_INNER
    flat_e = expert.reshape(nk)
    chunk = _pick(nk, (256, 128, 64, 32, 16, 8, 4, 2, 1))
    onehot = (flat_e[:, None] == jnp.arange(n_experts, dtype=jnp.int32)[None, :]).astype(F32)
    onehot = onehot.reshape(nk // chunk, chunk, n_experts)
    incl = (jnp.arange(chunk)[:, None] >= jnp.arange(chunk)[None, :]).astype(F32)
    within = jnp.einsum('ts,cse->cte', incl, onehot)
    totals = within[:, -1, :]
    csum = (within + (jnp.cumsum(totals, axis=0) - totals)[:, None, :]).reshape(nk, n_experts).astype(jnp.int32)
    rank = jnp.take_along_axis(csum, flat_e[:, None], axis=1)[:, 0] - 1
    counts = csum[-1]
    padded = ((counts + blk - 1) // blk) * blk
    pends = jnp.cumsum(padded)
    pstarts = pends - padded
    dest = (pstarts[flat_e] + rank).astype(jnp.int32)
    total = nk + n_experts * blk
    nb = total // blk
    flat_tok = jnp.repeat(jnp.arange(n, dtype=jnp.int32), TOP_K_INNER)
    rows = jnp.zeros((total,), jnp.int32).at[dest].set(flat_tok)
    blk_e = jnp.clip(jnp.searchsorted(pends, jnp.arange(nb, dtype=jnp.int32) * blk, side='right'),
                     0, n_experts - 1).astype(jnp.int32)
    return gates, dest.reshape(n, TOP_K_INNER), rows.reshape(nb, 1, blk), blk_e


def kernel(x, p, g_attn, w_in, g_sb, g_diff, lam_q1, lam_k1, lam_q2, lam_k2, w_out, g_moe, w_router_group,
           w_router_expert, w_gate, w_up, w_down, g_ple, w_ple, w_ple_gate, g_final):
    b, s, d = x.shape
    n = b * s
    depth = p.shape[0]
    sb_width = d // 2
    n_heads = sb_width // HEAD_DIM
    n_groups = w_router_group.shape[2]
    n_experts = w_router_expert.shape[2]
    assert n_groups + n_experts <= LANES and w_in.shape[2] == 6 * sb_width
    bq = _pick(s, (512, 256, 128))
    bk = _pick(bq, (256, 128))
    moe_blk = 128
    attn_heads = _pick(n_heads, (4, 2, 1))
    tm_c = _pick(n, (256, 128))

    tables = _rope_tables(s)
    tri = (jnp.arange(bk)[None, :] > jnp.arange(bk)[:, None]).astype(BF16)
    row2 = lambda a: a.reshape(1, -1)
    blocks = sb_width // HEAD_DIM

    h = x.reshape(n, d)
    for i in range(depth):
        qkv = _qkv_proj(h, row2(g_attn[i]), w_in[i].astype(BF16), tables, s)
        qkv3 = qkv.reshape(b, s, -1)
        o_sb = _sb_attention(qkv3, tri, row2(g_sb[i]), n_heads=n_heads, col0=0, bq=bq, bk=bk,
                             heads=attn_heads)
        lam_init = 0.8 - 0.6 * math.exp(-0.3 * i)
        lams = (row2(lam_q1[i]), row2(lam_k1[i]), row2(lam_q2[i]), row2(lam_k2[i]))
        o_df = _diff_attention(qkv3, lams, row2(g_diff[i]), n_heads=n_heads, col0=3 * blocks, bq=bq,
                               bk=bk, heads=attn_heads, lam_init=lam_init)
        w_router = jnp.concatenate(
            [w_router_group[i], w_router_expert[i], jnp.zeros((d, LANES - n_groups - n_experts), F32)], axis=1)
        h1, m, logits = _outproj_router(h, o_sb.reshape(n, -1), o_df.reshape(n, -1), w_out[i].astype(BF16),
                                        row2(g_moe[i]), _split_bf16(w_router))
        gates, pos, rows3, blk_e = _route(logits, n_groups, n_experts, moe_blk)
        ys = _expert_ffn(m, rows3, blk_e, w_gate[i].astype(BF16), w_up[i].astype(BF16), w_down[i].astype(BF16))
        pos_a = pos[:, 0].reshape(n // tm_c, 1, tm_c)
        pos_b = pos[:, 1].reshape(n // tm_c, 1, tm_c)
        h = _combine_ple(ys, pos_a, pos_b, h1, gates, p[i].reshape(n, -1), row2(g_ple[i]), w_ple[i].astype(BF16),
                         w_ple_gate[i].astype(BF16), row2(g_final), final=(i == depth - 1))
    return h.reshape(b, s, d)
```

```python
import functools
import math

import jax
import jax.numpy as jnp
from jax import lax
from jax.experimental import pallas as pl
from jax.experimental.pallas import tpu as pltpu

EPS = 1e-6
HEAD_DIM = 128
DIFF_QK_DIM = 64
CHUNK = 64
ROPE_THETA = 500000.0
ROT_DIM = DIFF_QK_DIM // 4
TOP_K_INNER = 2
LOG2E = 1.4426950408889634
LANES = 128
BF16_SUBLANES = 16
V7X_VMEM_LIMIT = 56 * 1024 * 1024

F32 = jnp.float32
BF16 = jnp.bfloat16


def _pick(n, prefs):
    for t in prefs:
        if n % t == 0:
            return t
    raise ValueError(f"no tile in {prefs} divides {n}")


def _rms_scale(x):
    return lax.rsqrt(jnp.mean(x * x, axis=-1, keepdims=True) + EPS)


def _qkv_kernel(x_ref, g_ref, w_ref, cos_ref, sina_ref, sinb_ref, o_ref, a_ref, *,
                blocks_per_region, sb_scale, df_scale):
    j = pl.program_id(1)

    @pl.when(j == 0)
    def _():
        x = x_ref[...]
        a_ref[...] = (x * _rms_scale(x) * g_ref[...]).astype(BF16)

    acc = jnp.dot(a_ref[...], w_ref[...], preferred_element_type=F32)
    region = j // blocks_per_region
    tn = acc.shape[1]

    def rope_store(scale):
        for c in range(tn // LANES):
            blk = acc[:, c * LANES:(c + 1) * LANES]
            r = (blk * cos_ref[...]
                 + pltpu.roll(blk, LANES - ROT_DIM // 2, 1) * sina_ref[...]
                 + pltpu.roll(blk, ROT_DIM // 2, 1) * sinb_ref[...])
            o_ref[:, c * LANES:(c + 1) * LANES] = (r * scale).astype(o_ref.dtype)

    @pl.when(region == 0)
    def _():
        o_ref[...] = (acc * sb_scale).astype(o_ref.dtype)

    @pl.when(region == 3)
    def _():
        rope_store(df_scale)

    @pl.when(region == 4)
    def _():
        rope_store(1.0)

    @pl.when((region == 1) | (region == 2) | (region == 5))
    def _():
        o_ref[...] = acc.astype(o_ref.dtype)


def _rope_tables(seq):
    half = ROT_DIM // 2
    inv = ROPE_THETA ** (-jnp.arange(0, ROT_DIM, 2, dtype=F32) / ROT_DIM)
    ang = jnp.arange(seq, dtype=F32)[:, None] * inv[None, :]
    cos, sin = jnp.cos(ang), jnp.sin(ang)
    comp = jnp.concatenate([cos, cos, jnp.ones((seq, DIFF_QK_DIM - ROT_DIM), F32)], axis=1)
    zeros_h = jnp.zeros((seq, half), F32)
    zeros_r = jnp.zeros((seq, DIFF_QK_DIM - ROT_DIM), F32)
    sina = jnp.concatenate([-sin, zeros_h, zeros_r], axis=1)
    sinb = jnp.concatenate([zeros_h, sin, zeros_r], axis=1)
    rep = LANES // DIFF_QK_DIM
    return jnp.tile(comp, (1, rep)), jnp.tile(sina, (1, rep)), jnp.tile(sinb, (1, rep))


def _qkv_proj(x2, g, w_bf16, tables, seq):
    n, d = x2.shape
    cols = w_bf16.shape[1]
    region = cols // 6
    tm = _pick(seq, (1024, 512, 256, 128))
    tn = _pick(region, (512, 256, 128))
    pos_blocks = seq // tm
    kern = functools.partial(_qkv_kernel, blocks_per_region=region // tn,
                             sb_scale=LOG2E / math.sqrt(HEAD_DIM), df_scale=LOG2E / math.sqrt(DIFF_QK_DIM))
    tab_spec = pl.BlockSpec((tm, LANES), lambda i, j: (i % pos_blocks, 0))
    return pl.pallas_call(
        kern,
        grid=(n // tm, cols // tn),
        in_specs=[pl.BlockSpec((tm, d), lambda i, j: (i, 0)),
                  pl.BlockSpec((1, d), lambda i, j: (0, 0)),
                  pl.BlockSpec((d, tn), lambda i, j: (0, j)),
                  tab_spec, tab_spec, tab_spec],
        out_specs=pl.BlockSpec((tm, tn), lambda i, j: (i, j)),
        out_shape=jax.ShapeDtypeStruct((n, cols), BF16),
        scratch_shapes=[pltpu.VMEM((tm, d), BF16)],
        compiler_params=pltpu.CompilerParams(dimension_semantics=("parallel", "arbitrary"),
                                             vmem_limit_bytes=V7X_VMEM_LIMIT),
        name="qkv_proj",
    )(x2, g, w_bf16, *tables)


def _build_v_transposed(v_ref, vt_ref, heads):
    bk = vt_ref.shape[3]
    extra = vt_ref.shape[2] - HEAD_DIM

    def body(c, carry):
        off = pl.multiple_of(c * bk, bk)
        for g in range(heads):
            blk = v_ref[0, pl.ds(off, bk), g * HEAD_DIM:(g + 1) * HEAD_DIM]
            vt_ref[g, c, :HEAD_DIM, :] = blk.astype(F32).T.astype(BF16)
            if extra:
                vt_ref[g, c, HEAD_DIM:, :] = jnp.ones((extra, bk), BF16)
        return carry

    lax.fori_loop(0, vt_ref.shape[1], body, 0)


def _sweep_key_tiles(i, tiles_per_block, bk, tile, *, descending):
    first_local = i * tiles_per_block

    def local(s, c):
        r = tiles_per_block - 1 - s if descending else s
        tile(first_local + r, r * bk)
        return c

    def earlier(s, c):
        tile(first_local - 1 - s if descending else s, None)
        return c

    if descending:
        lax.fori_loop(0, tiles_per_block, local, 0)
        lax.fori_loop(0, first_local, earlier, 0)
    else:
        lax.fori_loop(0, first_local, earlier, 0)
        lax.fori_loop(0, tiles_per_block, local, 0)


def _sb_attn_kernel(q_ref, k_ref, v_ref, tri_ref, g_ref, o_ref, vt_ref, acc_ref, carry_ref, *, bq, bk, heads):
    i = pl.program_id(2)

    @pl.when(i == 0)
    def _():
        _build_v_transposed(v_ref, vt_ref, heads)

    q_t = [q_ref[0, :, g * HEAD_DIM:(g + 1) * HEAD_DIM].astype(F32).T.astype(BF16) for g in range(heads)]
    tri = tri_ref[...]
    hs = range(heads)

    def tile(j, key_base):
        off = pl.multiple_of(j * bk, bk)
        if key_base is not None:
            key = lax.broadcasted_iota(jnp.int32, (bk, bq), 0) + key_base
            qry = lax.broadcasted_iota(jnp.int32, (bk, bq), 1)
            mask = key < qry
        zs = [jnp.dot(k_ref[0, pl.ds(off, bk), g * HEAD_DIM:(g + 1) * HEAD_DIM], q_t[g],
                      preferred_element_type=F32) for g in hs]
        sps, log_betas, sp_row0 = [], [], []
        for z in zs:
            z = jnp.minimum(z, 126.0)
            neg_log_1mb = jnp.log2(1.0 + jnp.exp2(z))
            log_beta = z - neg_log_1mb
            if key_base is not None:
                neg_log_1mb = jnp.where(mask, neg_log_1mb, 0.0)
            sps.append(neg_log_1mb.astype(BF16))
            log_betas.append(log_beta)
            sp_row0.append(neg_log_1mb[0:1, :])
        sufs = [jnp.dot(tri, sp, preferred_element_type=F32) for sp in sps]
        for g in hs:
            a = jnp.exp2(log_betas[g] - sufs[g])
            if key_base is not None:
                a = jnp.where(mask, a, 0.0)
            pv = jnp.dot(vt_ref[g, j], a.astype(BF16), preferred_element_type=F32)
            acc_ref[g] += pv * jnp.exp2(-carry_ref[g])
        for g in hs:
            carry_ref[g] += sufs[g][0:1, :] + sp_row0[g]

    acc_ref[...] = jnp.zeros_like(acc_ref)
    carry_ref[...] = jnp.zeros_like(carry_ref)
    _sweep_key_tiles(i, bq // bk, bk, tile, descending=True)
    for g in range(heads):
        o = acc_ref[g].T
        o_ref[0, :, g * HEAD_DIM:(g + 1) * HEAD_DIM] = (o * _rms_scale(o) * g_ref[...]).astype(o_ref.dtype)


def _attn_specs(b, s, n_heads, col0, bq, heads):
    width = heads * HEAD_DIM
    c0, groups = col0 // heads, n_heads // heads
    return ([pl.BlockSpec((1, bq, width), lambda bi, h, i: (bi, i, c0 + h)),
             pl.BlockSpec((1, s, width), lambda bi, h, i: (bi, 0, c0 + groups + h)),
             pl.BlockSpec((1, s, width), lambda bi, h, i: (bi, 0, c0 + 2 * groups + h))],
            pl.BlockSpec((1, bq, width), lambda bi, h, i: (bi, i, h)),
            (b, groups, s // bq))


def _sb_attention(qkv3, tri, g, *, n_heads, col0, bq, bk, heads):
    b, s, _ = qkv3.shape
    qkv_specs, out_spec, grid = _attn_specs(b, s, n_heads, col0, bq, heads)
    return pl.pallas_call(
        functools.partial(_sb_attn_kernel, bq=bq, bk=bk, heads=heads),
        grid=grid,
        in_specs=qkv_specs + [pl.BlockSpec((bk, bk), lambda bi, h, i: (0, 0)),
                              pl.BlockSpec((1, HEAD_DIM), lambda bi, h, i: (0, 0))],
        out_specs=out_spec,
        out_shape=jax.ShapeDtypeStruct((b, s, n_heads * HEAD_DIM), BF16),
        scratch_shapes=[pltpu.VMEM((heads, s // bk, HEAD_DIM, bk), BF16),
                        pltpu.VMEM((heads, HEAD_DIM, bq), F32), pltpu.VMEM((heads, 1, bq), F32)],
        compiler_params=pltpu.CompilerParams(dimension_semantics=("parallel", "parallel", "arbitrary"),
                                             vmem_limit_bytes=V7X_VMEM_LIMIT),
        name="sb_attention",
    )(qkv3, qkv3, qkv3, tri, g)


def _diff_attn_kernel(q_ref, k_ref, v_ref, lq1_ref, lk1_ref, lq2_ref, lk2_ref, g_ref, o_ref,
                      vt_ref, acc_ref, m_ref, l_ref, *, bq, bk, heads, lam_init):
    i = pl.program_id(2)

    @pl.when(i == 0)
    def _():
        _build_v_transposed(v_ref, vt_ref, heads)

    dim = lax.broadcasted_iota(jnp.int32, (HEAD_DIM, bq), 0)
    q_t = []
    for g in range(heads):
        qt = q_ref[0, :, g * HEAD_DIM:(g + 1) * HEAD_DIM].astype(F32).T
        q_t.append((jnp.where(dim < DIFF_QK_DIM, qt, 0.0).astype(BF16),
                    jnp.where(dim >= DIFF_QK_DIM, qt, 0.0).astype(BF16)))

    chains = [(g, c) for g in range(heads) for c in range(2)]

    def tile(j, key_base):
        off = pl.multiple_of(j * bk, bk)
        if key_base is not None:
            key = lax.broadcasted_iota(jnp.int32, (bk, bq), 0) + key_base
            qry = lax.broadcasted_iota(jnp.int32, (bk, bq), 1)
            mask = (key // CHUNK) <= (qry // CHUNK)
        ss = []
        for g, c in chains:
            s = jnp.dot(k_ref[0, pl.ds(off, bk), g * HEAD_DIM:(g + 1) * HEAD_DIM], q_t[g][c],
                        preferred_element_type=F32)
            ss.append(jnp.where(mask, s, -1e30) if key_base is not None else s)
        ps, alphas = [], []
        for (g, c), s in zip(chains, ss):
            m_prev = m_ref[g, c]
            m_new = jnp.maximum(m_prev, jnp.max(s, axis=0, keepdims=True))
            alpha = jnp.exp2(m_prev - m_new)
            m_ref[g, c] = m_new
            ps.append(jnp.exp2(s - m_new).astype(BF16))
            alphas.append(alpha)
        pvs = [jnp.dot(vt_ref[g, j], p, preferred_element_type=F32) for (g, c), p in zip(chains, ps)]
        for (g, c), alpha, pv in zip(chains, alphas, pvs):
            acc_ref[g, c] = alpha * acc_ref[g, c] + pv[:HEAD_DIM, :]
            l_ref[g, c] = alpha * l_ref[g, c] + pv[HEAD_DIM:HEAD_DIM + 1, :]

    acc_ref[...] = jnp.zeros_like(acc_ref)
    m_ref[...] = jnp.full_like(m_ref, -1e30)
    l_ref[...] = jnp.zeros_like(l_ref)
    _sweep_key_tiles(i, bq // bk, bk, tile, descending=False)
    lam = (jnp.exp(jnp.sum(lq1_ref[...] * lk1_ref[...], axis=-1, keepdims=True))
           - jnp.exp(jnp.sum(lq2_ref[...] * lk2_ref[...], axis=-1, keepdims=True)) + lam_init)
    for g in range(heads):
        o = (acc_ref[g, 0] / l_ref[g, 0] - lam * (acc_ref[g, 1] / l_ref[g, 1])).T
        o_ref[0, :, g * HEAD_DIM:(g + 1) * HEAD_DIM] = (
            o * _rms_scale(o) * g_ref[...] * (1.0 - lam_init)).astype(o_ref.dtype)


def _diff_attention(qkv3, lams, g, *, n_heads, col0, bq, bk, heads, lam_init):
    b, s, _ = qkv3.shape
    qkv_specs, out_spec, grid = _attn_specs(b, s, n_heads, col0, bq, heads)
    vec = lambda w: pl.BlockSpec((1, w), lambda bi, h, i: (0, 0))
    return pl.pallas_call(
        functools.partial(_diff_attn_kernel, bq=bq, bk=bk, heads=heads, lam_init=lam_init),
        grid=grid,
        in_specs=qkv_specs + [vec(DIFF_QK_DIM), vec(DIFF_QK_DIM), vec(DIFF_QK_DIM), vec(DIFF_QK_DIM), vec(HEAD_DIM)],
        out_specs=out_spec,
        out_shape=jax.ShapeDtypeStruct((b, s, n_heads * HEAD_DIM), BF16),
        scratch_shapes=[pltpu.VMEM((heads, s // bk, HEAD_DIM + BF16_SUBLANES, bk), BF16),
                        pltpu.VMEM((heads, 2, HEAD_DIM, bq), F32),
                        pltpu.VMEM((heads, 2, 1, bq), F32), pltpu.VMEM((heads, 2, 1, bq), F32)],
        compiler_params=pltpu.CompilerParams(dimension_semantics=("parallel", "parallel", "arbitrary"),
                                             vmem_limit_bytes=V7X_VMEM_LIMIT),
        name="diff_attention",
    )(qkv3, qkv3, qkv3, *lams, g)


def _outproj_kernel(x_ref, osb_ref, odf_ref, w_ref, g_ref, wr_ref, h_ref, m_ref, lg_ref, *, sb_width):
    h = (x_ref[...]
         + jnp.dot(osb_ref[...], w_ref[:sb_width, :], preferred_element_type=F32)
         + jnp.dot(odf_ref[...], w_ref[sb_width:, :], preferred_element_type=F32))
    h_ref[...] = h
    m = h * _rms_scale(h) * g_ref[...]
    m_ref[...] = m
    m_hi = m.astype(BF16)
    m_lo = (m - m_hi.astype(F32)).astype(BF16)
    hi_part = jnp.dot(m_hi, wr_ref[...], preferred_element_type=F32)
    lo_part = jnp.dot(m_lo, wr_ref[:, :LANES], preferred_element_type=F32)
    lg_ref[...] = hi_part[:, :LANES] + (hi_part[:, LANES:] + lo_part)


def _split_bf16(w):
    hi = w.astype(BF16)
    return jnp.concatenate([hi, (w - hi.astype(F32)).astype(BF16)], axis=1)


def _outproj_router(x2, o_sb, o_df, w_out_bf16, g_moe, w_router):
    n, d = x2.shape
    sbw, dfw = o_sb.shape[1], o_df.shape[1]
    tm = _pick(n, (512, 256, 128))
    row = lambda w: pl.BlockSpec((tm, w), lambda i: (i, 0))
    full = lambda a: pl.BlockSpec(a.shape, lambda i: (0, 0))
    return pl.pallas_call(
        functools.partial(_outproj_kernel, sb_width=sbw),
        grid=(n // tm,),
        in_specs=[row(d), row(sbw), row(dfw), full(w_out_bf16), full(g_moe), full(w_router)],
        out_specs=[row(d), row(d), row(LANES)],
        out_shape=[jax.ShapeDtypeStruct((n, d), F32), jax.ShapeDtypeStruct((n, d), F32),
                   jax.ShapeDtypeStruct((n, LANES), F32)],
        compiler_params=pltpu.CompilerParams(dimension_semantics=("parallel",),
                                             vmem_limit_bytes=V7X_VMEM_LIMIT),
        name="outproj_router",
    )(x2, o_sb, o_df, w_out_bf16, g_moe, w_router)


def _start_row_gather(idx_ref, src_hbm, dst_ref, sem):
    def issue(r, c):
        pltpu.make_async_copy(src_hbm.at[pl.ds(idx_ref[0, 0, r], 1), :], dst_ref.at[pl.ds(r, 1), :], sem).start()
        return c

    lax.fori_loop(0, dst_ref.shape[0], issue, 0, unroll=8)


def _wait_row_gather(src_hbm, dst_ref, sem):
    pltpu.make_async_copy(src_hbm.at[pl.ds(0, dst_ref.shape[0]), :], dst_ref, sem).wait()


def _prefetched_gathers(gathers, consume, compute):
    i = pl.program_id(0)

    @pl.when(i == 0)
    def _():
        for idx_now, _, src, buf, sem in gathers:
            _start_row_gather(idx_now, src, buf, sem)

    for _, _, src, buf, sem in gathers:
        _wait_row_gather(src, buf, sem)
    consume()
    for _, idx_next, src, buf, sem in gathers:
        _start_row_gather(idx_next, src, buf, sem)
    compute()

    @pl.when(i == pl.num_programs(0) - 1)
    def _():
        for _, _, src, buf, sem in gathers:
            _wait_row_gather(src, buf, sem)


def _expert_kernel(be_ref, rows_ref, rows_next_ref, m_hbm, wg_ref, wu_ref, wd_ref, y_ref,
                   x_buf, xb_ref, wg_bf, wu_bf, wd_bf, sem):
    i = pl.program_id(0)

    @pl.when((i == 0) | (be_ref[i] != be_ref[jnp.maximum(i - 1, 0)]))
    def _():
        wg_bf[...] = wg_ref[0].astype(BF16)
        wu_bf[...] = wu_ref[0].astype(BF16)
        wd_bf[...] = wd_ref[0].astype(BF16)

    def consume():
        xb_ref[...] = x_buf[...].astype(BF16)

    def compute():
        xb = xb_ref[...]
        g = jnp.dot(xb, wg_bf[...], preferred_element_type=F32)
        u = jnp.dot(xb, wu_bf[...], preferred_element_type=F32)
        hmid = g * (1.0 / (1.0 + jnp.exp(-g))) * u
        y_ref[...] = jnp.dot(hmid.astype(BF16), wd_bf[...], preferred_element_type=F32)

    _prefetched_gathers([(rows_ref, rows_next_ref, m_hbm, x_buf, sem)], consume, compute)


def _expert_ffn(m, rows3, blk_e, wg, wu, wd):
    n, d = m.shape
    nb, _, blk = rows3.shape
    de = wg.shape[2]
    grid_spec = pltpu.PrefetchScalarGridSpec(
        num_scalar_prefetch=1,
        grid=(nb,),
        in_specs=[pl.BlockSpec((1, 1, blk), lambda i, be: (i, 0, 0), memory_space=pltpu.SMEM),
                  pl.BlockSpec((1, 1, blk), lambda i, be: (jnp.minimum(i + 1, nb - 1), 0, 0),
                               memory_space=pltpu.SMEM),
                  pl.BlockSpec(memory_space=pl.ANY),
                  pl.BlockSpec((1, d, de), lambda i, be: (be[i], 0, 0)),
                  pl.BlockSpec((1, d, de), lambda i, be: (be[i], 0, 0)),
                  pl.BlockSpec((1, de, d), lambda i, be: (be[i], 0, 0))],
        out_specs=pl.BlockSpec((blk, d), lambda i, be: (i, 0)),
        scratch_shapes=[pltpu.VMEM((blk, d), F32), pltpu.VMEM((blk, d), BF16),
                        pltpu.VMEM((d, de), BF16), pltpu.VMEM((d, de), BF16), pltpu.VMEM((de, d), BF16),
                        pltpu.SemaphoreType.DMA(())],
    )
    return pl.pallas_call(
        _expert_kernel,
        grid_spec=grid_spec,
        out_shape=jax.ShapeDtypeStruct((nb * blk, d), F32),
        compiler_params=pltpu.CompilerParams(dimension_semantics=("arbitrary",),
                                             vmem_limit_bytes=V7X_VMEM_LIMIT),
        name="expert_ffn",
    )(blk_e, rows3, rows3, m, wg, wu, wd)


def _combine_kernel(pa_ref, pa_next_ref, pb_ref, pb_next_ref, ys_hbm, h_ref, gates_ref, p_ref, gple_ref, wple_ref,
                    wgate_ref, gfin_ref, o_ref, ya_buf, yb_buf, h_buf, sem_a, sem_b, *, final):
    def consume():
        gates = gates_ref[...]
        h_buf[...] = h_ref[...] + gates[:, 0:1] * ya_buf[...] + gates[:, 1:2] * yb_buf[...]

    def compute():
        h = h_buf[...]
        e = (h * _rms_scale(h) * gple_ref[...]).astype(BF16)
        z = jnp.dot(e, wgate_ref[...], preferred_element_type=F32)
        pe = jnp.dot(p_ref[...].astype(BF16), wple_ref[...], preferred_element_type=F32)
        h = h + pe * (1.0 / (1.0 + jnp.exp(-z)))
        if final:
            h = h * _rms_scale(h) * gfin_ref[...]
        o_ref[...] = h

    _prefetched_gathers([(pa_ref, pa_next_ref, ys_hbm, ya_buf, sem_a),
                         (pb_ref, pb_next_ref, ys_hbm, yb_buf, sem_b)], consume, compute)


def _combine_ple(ys, pos_a, pos_b, h1, gates, p2, g_ple, w_ple, w_ple_gate, g_final, *, final):
    n, d = h1.shape
    nt, _, tm = pos_a.shape
    row = lambda w: pl.BlockSpec((tm, w), lambda i: (i, 0))
    full = lambda a: pl.BlockSpec(a.shape, lambda i: (0, 0))
    idx = pl.BlockSpec((1, 1, tm), lambda i: (i, 0, 0), memory_space=pltpu.SMEM)
    idx_next = pl.BlockSpec((1, 1, tm), lambda i: (jnp.minimum(i + 1, nt - 1), 0, 0), memory_space=pltpu.SMEM)
    return pl.pallas_call(
        functools.partial(_combine_kernel, final=final),
        grid=(nt,),
        in_specs=[idx, idx_next, idx, idx_next, pl.BlockSpec(memory_space=pl.ANY), row(d), row(gates.shape[1]),
                  row(p2.shape[1]), full(g_ple), full(w_ple), full(w_ple_gate), full(g_final)],
        out_specs=row(d),
        out_shape=jax.ShapeDtypeStruct((n, d), F32),
        scratch_shapes=[pltpu.VMEM((tm, d), F32), pltpu.VMEM((tm, d), F32), pltpu.VMEM((tm, d), F32),
                        pltpu.SemaphoreType.DMA(()), pltpu.SemaphoreType.DMA(())],
        compiler_params=pltpu.CompilerParams(dimension_semantics=("arbitrary",),
                                             vmem_limit_bytes=V7X_VMEM_LIMIT),
        name="combine_ple",
    )(pos_a, pos_a, pos_b, pos_b, ys, h1, gates, p2, g_ple, w_ple, w_ple_gate, g_final)


def _route(logits, n_groups, n_experts, blk):
    n = logits.shape[0]
    epg = n_experts // n_groups
    p_grp = jax.nn.softmax(logits[:, :n_groups], axis=-1)
    grp = jnp.argmax(p_grp, axis=-1)
    p_sel = jnp.take_along_axis(p_grp, grp[:, None], axis=1)[:, 0]
    e_logits = logits[:, n_groups:n_groups + n_experts].reshape(n, n_groups, epg)
    in_grp = jnp.take_along_axis(e_logits, grp[:, None, None], axis=1)[:, 0]
    top_v, top_i = lax.top_k(in_grp, TOP_K_INNER)
    gates = jax.nn.softmax(top_v, axis=-1) * p_sel[:, None]
    expert = (grp[:, None] * epg + top_i).astype(jnp.int32)

    nk = n * TOP_K_INNER
    flat_e = expert.reshape(nk)
    chunk = _pick(nk, (256, 128, 64, 32, 16, 8, 4, 2, 1))
    onehot = (flat_e[:, None] == jnp.arange(n_experts, dtype=jnp.int32)[None, :]).astype(F32)
    onehot = onehot.reshape(nk // chunk, chunk, n_experts)
    incl = (jnp.arange(chunk)[:, None] >= jnp.arange(chunk)[None, :]).astype(F32)
    within = jnp.einsum('ts,cse->cte', incl, onehot)
    totals = within[:, -1, :]
    csum = (within + (jnp.cumsum(totals, axis=0) - totals)[:, None, :]).reshape(nk, n_experts).astype(jnp.int32)
    rank = jnp.take_along_axis(csum, flat_e[:, None], axis=1)[:, 0] - 1
    counts = csum[-1]
    padded = ((counts + blk - 1) // blk) * blk
    pends = jnp.cumsum(padded)
    pstarts = pends - padded
    dest = (pstarts[flat_e] + rank).astype(jnp.int32)
    total = nk + n_experts * blk
    nb = total // blk
    flat_tok = jnp.repeat(jnp.arange(n, dtype=jnp.int32), TOP_K_INNER)
    rows = jnp.zeros((total,), jnp.int32).at[dest].set(flat_tok)
    blk_start = jnp.arange(nb, dtype=jnp.int32) * blk
    blk_e = jnp.minimum(jnp.sum((pends[None, :] <= blk_start[:, None]).astype(jnp.int32), axis=1), n_experts - 1)
    return gates, dest.reshape(n, TOP_K_INNER), rows.reshape(nb, 1, blk), blk_e


def kernel(x, p, g_attn, w_in, g_sb, g_diff, lam_q1, lam_k1, lam_q2, lam_k2, w_out, g_moe, w_router_group,
           w_router_expert, w_gate, w_up, w_down, g_ple, w_ple, w_ple_gate, g_final):
    b, s, d = x.shape
    n = b * s
    depth = p.shape[0]
    sb_width = d // 2
    n_heads = sb_width // HEAD_DIM
    n_groups = w_router_group.shape[2]
    n_experts = w_router_expert.shape[2]
    assert n_groups + n_experts <= LANES and w_in.shape[2] == 6 * sb_width
    bq = _pick(s, (512, 256, 128))
    bk = _pick(bq, (256, 128))
    moe_blk = 128
    attn_heads = _pick(n_heads, (4, 2, 1))
    tm_c = _pick(n, (256, 128))

    tables = _rope_tables(s)
    tri = (jnp.arange(bk)[None, :] > jnp.arange(bk)[:, None]).astype(BF16)
    row2 = lambda a: a.reshape(1, -1)
    blocks = sb_width // HEAD_DIM

    h = x.reshape(n, d)
    for i in range(depth):
        qkv = _qkv_proj(h, row2(g_attn[i]), w_in[i].astype(BF16), tables, s)
        qkv3 = qkv.reshape(b, s, -1)
        o_sb = _sb_attention(qkv3, tri, row2(g_sb[i]), n_heads=n_heads, col0=0, bq=bq, bk=bk,
                             heads=attn_heads)
        lam_init = 0.8 - 0.6 * math.exp(-0.3 * i)
        lams = (row2(lam_q1[i]), row2(lam_k1[i]), row2(lam_q2[i]), row2(lam_k2[i]))
        o_df = _diff_attention(qkv3, lams, row2(g_diff[i]), n_heads=n_heads, col0=3 * blocks, bq=bq,
                               bk=bk, heads=attn_heads, lam_init=lam_init)
        w_router = jnp.concatenate(
            [w_router_group[i], w_router_expert[i], jnp.zeros((d, LANES - n_groups - n_experts), F32)], axis=1)
        h1, m, logits = _outproj_router(h, o_sb.reshape(n, -1), o_df.reshape(n, -1), w_out[i].astype(BF16),
                                        row2(g_moe[i]), _split_bf16(w_router))
        gates, pos, rows3, blk_e = _route(logits, n_groups, n_experts, moe_blk)
        ys = _expert_ffn(m, rows3, blk_e, w_gate[i], w_up[i], w_down[i])
        pos_a = pos[:, 0].reshape(n // tm_c, 1, tm_c)
        pos_b = pos[:, 1].reshape(n // tm_c, 1, tm_c)
        h = _combine_ple(ys, pos_a, pos_b, h1, gates, p[i].reshape(n, -1), row2(g_ple[i]), w_ple[i].astype(BF16),
                         w_ple_gate[i].astype(BF16), row2(g_final), final=(i == depth - 1))
    return h.reshape(b, s, d)
```
